```python
import jax, jax.numpy as jnp
from jax import lax
import numpy as np

D_MODEL = 2048
BATCH = 2
SEQ = 4096
DEPTH = 1

MIX_WIDTH = D_MODEL
FOURIER_HEAD_DIM = 128
FOURIER_HEADS = (MIX_WIDTH // 2) // FOURIER_HEAD_DIM
FOURIER_WIDTH = FOURIER_HEADS * FOURIER_HEAD_DIM
GMLP_HEAD_DIM = 128
GMLP_HEADS = (MIX_WIDTH - FOURIER_WIDTH) // GMLP_HEAD_DIM
GMLP_WIDTH = GMLP_HEADS * GMLP_HEAD_DIM
CHUNK = 128
IN_WIDTH = FOURIER_WIDTH + 2 * GMLP_WIDTH
D_FF = (-(-8 * D_MODEL // (3 * 256))) * 256
EPS = 1e-6

kernel_name = "hybrid_fnet_gmlp_encoder_block"


def rmsnorm(x, g):
    xf = x.astype(jnp.float32)
    y = xf * lax.rsqrt(jnp.mean(xf * xf, axis=-1, keepdims=True) + EPS)
    return (y * g.astype(jnp.float32)).astype(x.dtype)


def fourier_mixer(a, w_fourier):
    b, s, _ = a.shape
    a4 = a.reshape(b, s, FOURIER_HEADS, FOURIER_HEAD_DIM).astype(jnp.float32)
    f = jnp.fft.fftn(a4, axes=(1, 3), norm="ortho").real.astype(a.dtype)
    f = jnp.einsum("bshd,hde->bshe", f, w_fourier)
    return f.reshape(b, s, FOURIER_WIDTH)


def spatial_gating_mixer(u, v, g_sgu, w_spatial, b_spatial):
    b, s, _ = v.shape
    v = rmsnorm(v, g_sgu)
    v5 = v.reshape(b, s // CHUNK, CHUNK, GMLP_HEADS, GMLP_HEAD_DIM)
    sv = jnp.einsum("hpq,bnqhd->bnphd", w_spatial, v5)
    sv = sv + jnp.transpose(b_spatial)[None, None, :, :, None]
    return u * sv.reshape(b, s, GMLP_WIDTH)


def setup_inputs(seed: int = 0) -> dict:
    key = jax.random.key(seed)
    ks = jax.random.split(key, 16)
    f32 = jnp.float32
    L = DEPTH
    x = jax.random.normal(ks[0], (BATCH, SEQ, D_MODEL), f32)
    norm_mix = 1.0 + 0.02 * jax.random.normal(ks[1], (L, D_MODEL), f32)
    w_in = jax.random.normal(ks[2], (L, D_MODEL, IN_WIDTH), f32) * D_MODEL ** -0.5
    w_fourier = jax.random.normal(ks[3], (L, FOURIER_HEADS, FOURIER_HEAD_DIM, FOURIER_HEAD_DIM), f32) * FOURIER_HEAD_DIM ** -0.5
    sgu_norm = 1.0 + 0.02 * jax.random.normal(ks[4], (L, GMLP_WIDTH), f32)
    w_spatial = jax.random.normal(ks[5], (L, GMLP_HEADS, CHUNK, CHUNK), f32) * CHUNK ** -0.5
    b_spatial = 1.0 + 0.02 * jax.random.normal(ks[6], (L, GMLP_HEADS, CHUNK), f32)
    w_out = jax.random.normal(ks[7], (L, MIX_WIDTH, D_MODEL), f32) * MIX_WIDTH ** -0.5
    norm_ffn = 1.0 + 0.02 * jax.random.normal(ks[8], (L, D_MODEL), f32)
    w_gate = jax.random.normal(ks[9], (L, D_MODEL, D_FF), f32) * D_MODEL ** -0.5
    w_up = jax.random.normal(ks[10], (L, D_MODEL, D_FF), f32) * D_MODEL ** -0.5
    w_down = jax.random.normal(ks[11], (L, D_FF, D_MODEL), f32) * D_FF ** -0.5
    norm_final = 1.0 + 0.02 * jax.random.normal(ks[12], (D_MODEL,), f32)
    return {"x": x, "norm_mix": norm_mix, "w_in": w_in, "w_fourier": w_fourier,
            "sgu_norm": sgu_norm, "w_spatial": w_spatial, "b_spatial": b_spatial,
            "w_out": w_out, "norm_ffn": norm_ffn, "w_gate": w_gate, "w_up": w_up,
            "w_down": w_down, "norm_final": norm_final}


def reference(x, norm_mix, w_in, w_fourier, sgu_norm, w_spatial, b_spatial, w_out,
              norm_ffn, w_gate, w_up, w_down, norm_final):
    for l in range(DEPTH):
        h = rmsnorm(x, norm_mix[l])
        z = h @ w_in[l]
        a = z[..., :FOURIER_WIDTH]
        uv = jax.nn.gelu(z[..., FOURIER_WIDTH:])
        u = uv[..., :GMLP_WIDTH]
        v = uv[..., GMLP_WIDTH:]
        y_f = fourier_mixer(a, w_fourier[l])
        y_g = spatial_gating_mixer(u, v, sgu_norm[l], w_spatial[l], b_spatial[l])
        x = x + jnp.concatenate([y_f, y_g], axis=-1) @ w_out[l]
        h2 = rmsnorm(x, norm_ffn[l])
        x = x + (jax.nn.silu(h2 @ w_gate[l]) * (h2 @ w_up[l])) @ w_down[l]
    return rmsnorm(x, norm_final)
```

```python
import functools
import math

import jax
import jax.numpy as jnp
from jax import lax
from jax.experimental import pallas as pl
from jax.experimental.pallas import tpu as pltpu

EPS = 1e-6
HEAD = 128
CHUNK = 128
F32 = jnp.float32
BF16 = jnp.bfloat16

VMEM_LIMIT_BYTES = 60000 * 1024


def _dft_mats(n):
    idx = jnp.arange(n, dtype=jnp.int32)
    ks = (idx[:, None] * idx[None, :]) % n
    ang = ks.astype(F32) * (2.0 * math.pi / n)
    scale = 1.0 / math.sqrt(n)
    return jnp.cos(ang) * scale, jnp.sin(ang) * scale


def _rms(x, g):
    ms = jnp.mean(x * x, axis=-1, keepdims=True)
    return x * lax.rsqrt(ms + EPS) * g


def _gelu_tanh(x):
    c = math.sqrt(2.0 / math.pi)
    return 0.5 * x * (1.0 + jnp.tanh(c * (x + 0.044715 * (x * x * x))))


def _fold_kernel(cd_ref, sd_ref, w_ref, o_ref):
    w = w_ref[0]
    wc = jnp.dot(cd_ref[...], w, preferred_element_type=F32, precision=lax.Precision.HIGHEST)
    ws = jnp.dot(sd_ref[...], w, preferred_element_type=F32, precision=lax.Precision.HIGHEST)
    o_ref[0, :, :HEAD] = wc.astype(BF16)
    o_ref[0, :, HEAD:] = ws.astype(BF16)


def _fold_fourier_weights(w_fourier):
    nh = w_fourier.shape[0]
    cd, sd = _dft_mats(HEAD)
    return pl.pallas_call(
        _fold_kernel,
        grid=(nh,),
        in_specs=[pl.BlockSpec((HEAD, HEAD), lambda h: (0, 0)),
                  pl.BlockSpec((HEAD, HEAD), lambda h: (0, 0)),
                  pl.BlockSpec((1, HEAD, HEAD), lambda h: (h, 0, 0))],
        out_specs=pl.BlockSpec((1, HEAD, 2 * HEAD), lambda h: (h, 0, 0)),
        out_shape=jax.ShapeDtypeStruct((nh, HEAD, 2 * HEAD), BF16),
        name="fold_fourier_weights",
    )(cd, sd, w_fourier)


def _mix_in_kernel(x_ref, gmix_ref, win_ref, wcs_ref, gsgu_ref, wsp_ref, bsp_ref,
                   p_ref, q_ref, yg_ref, z_ref, *, fw, gw):
    tm = x_ref.shape[0]
    nheads_f = fw // HEAD
    nheads_g = gw // HEAD
    nchunk = tm // CHUNK

    h = _rms(x_ref[...], gmix_ref[...]).astype(BF16)
    z_ref[...] = jnp.dot(h, win_ref[...], preferred_element_type=F32)

    for hd in range(nheads_f):
        cols = slice(hd * HEAD, (hd + 1) * HEAD)
        pq = jnp.dot(z_ref[:, cols].astype(BF16), wcs_ref[hd], preferred_element_type=F32)
        p_ref[:, cols] = pq[:, :HEAD].astype(BF16)
        q_ref[:, cols] = pq[:, HEAD:].astype(BF16)

    v = _rms(_gelu_tanh(z_ref[:, fw + gw:]), gsgu_ref[...]).astype(BF16)
    for hd in range(nheads_g):
        cols = slice(hd * HEAD, (hd + 1) * HEAD)
        vcat = jnp.concatenate(
            [v[n * CHUNK:(n + 1) * CHUNK, cols] for n in range(nchunk)], axis=1)
        sv = jnp.dot(wsp_ref[hd], vcat, preferred_element_type=F32)
        bias = bsp_ref[:, cols]
        for n in range(nchunk):
            rows = slice(n * CHUNK, (n + 1) * CHUNK)
            u = _gelu_tanh(z_ref[rows, fw + hd * HEAD: fw + (hd + 1) * HEAD])
            yg_ref[rows, cols] = (u * (sv[:, n * HEAD:(n + 1) * HEAD] + bias)).astype(BF16)


def _mix_in(x2, g_mix, w_in, wcs, g_sgu, w_sp, b_full, *, tm):
    t, d = x2.shape
    nin = w_in.shape[1]
    gw = g_sgu.shape[1]
    fw = nin - 2 * gw
    const2 = lambda i: (0, 0)
    const3 = lambda i: (0, 0, 0)
    single = pl.Buffered(1)
    return pl.pallas_call(
        functools.partial(_mix_in_kernel, fw=fw, gw=gw),
        grid=(t // tm,),
        in_specs=[
            pl.BlockSpec((tm, d), lambda i: (i, 0)),
            pl.BlockSpec((1, d), const2),
            pl.BlockSpec((d, nin), const2, pipeline_mode=single),
            pl.BlockSpec(wcs.shape, const3),
            pl.BlockSpec((1, gw), const2),
            pl.BlockSpec(w_sp.shape, const3),
            pl.BlockSpec(b_full.shape, const2),
        ],
        out_specs=[
            pl.BlockSpec((tm, fw), lambda i: (i, 0)),
            pl.BlockSpec((tm, fw), lambda i: (i, 0)),
            pl.BlockSpec((tm, gw), lambda i: (i, 0)),
        ],
        out_shape=[
            jax.ShapeDtypeStruct((t, fw), BF16),
            jax.ShapeDtypeStruct((t, fw), BF16),
            jax.ShapeDtypeStruct((t, gw), BF16),
        ],
        scratch_shapes=[pltpu.VMEM((tm, nin), F32)],
        compiler_params=pltpu.CompilerParams(
            dimension_semantics=("arbitrary",), vmem_limit_bytes=VMEM_LIMIT_BYTES),
        name="mix_in",
    )(x2, g_mix, w_in, wcs, g_sgu, w_sp, b_full)


def _seq_dft_kernel(cs_ref, ss_ref, p_ref, q_ref, o_ref):
    acc = jnp.dot(cs_ref[...], p_ref[0], preferred_element_type=F32)
    acc -= jnp.dot(ss_ref[...], q_ref[0], preferred_element_type=F32)
    o_ref[0] = acc.astype(o_ref.dtype)


def _seq_dft(p3, q3, cs, ss, *, tm, tn):
    b, s, w = p3.shape
    return pl.pallas_call(
        _seq_dft_kernel,
        grid=(b, w // tn, s // tm),
        in_specs=[
            pl.BlockSpec((tm, s), lambda bi, j, i: (i, 0)),
            pl.BlockSpec((tm, s), lambda bi, j, i: (i, 0)),
            pl.BlockSpec((1, s, tn), lambda bi, j, i: (bi, 0, j)),
            pl.BlockSpec((1, s, tn), lambda bi, j, i: (bi, 0, j)),
        ],
        out_specs=pl.BlockSpec((1, tm, tn), lambda bi, j, i: (bi, i, j)),
        out_shape=jax.ShapeDtypeStruct((b, s, w), BF16),
        compiler_params=pltpu.CompilerParams(
            dimension_semantics=("arbitrary", "arbitrary", "arbitrary"),
            vmem_limit_bytes=VMEM_LIMIT_BYTES),
        name="seq_dft",
    )(cs, ss, p3, q3)


def _out_ffn_kernel(x_ref, yf_ref, yg_ref, wo_ref, gffn_ref, wg_ref, wu_ref, wd_ref, gfin_ref,
                    o_ref, x1_ref, h2_ref, acc_ref):
    j = pl.program_id(1)
    fw = yf_ref.shape[1]

    @pl.when(j == 0)
    def _():
        x1 = x_ref[...]
        x1 = x1 + jnp.dot(yf_ref[...], wo_ref[:fw, :], preferred_element_type=F32)
        x1 = x1 + jnp.dot(yg_ref[...], wo_ref[fw:, :], preferred_element_type=F32)
        x1_ref[...] = x1
        h2_ref[...] = _rms(x1, gffn_ref[...]).astype(BF16)
        acc_ref[...] = jnp.zeros_like(acc_ref)

    h2 = h2_ref[...]
    g = jnp.dot(h2, wg_ref[...], preferred_element_type=F32)
    u = jnp.dot(h2, wu_ref[...], preferred_element_type=F32)
    t = (g * jax.nn.sigmoid(g) * u).astype(BF16)
    acc_ref[...] += jnp.dot(t, wd_ref[...], preferred_element_type=F32)

    @pl.when(j == pl.num_programs(1) - 1)
    def _():
        o_ref[...] = _rms(x1_ref[...] + acc_ref[...], gfin_ref[...])


def _out_ffn(x2, yf, yg, w_out, g_ffn, w_gate, w_up, w_down, g_fin, *, tm, tf):
    t, d = x2.shape
    fw = yf.shape[1]
    gw = yg.shape[1]
    dff = w_gate.shape[1]
    row = lambda i, j: (i, 0)
    const = lambda i, j: (0, 0)
    return pl.pallas_call(
        _out_ffn_kernel,
        grid=(t // tm, dff // tf),
        in_specs=[
            pl.BlockSpec((tm, d), row),
            pl.BlockSpec((tm, fw), row),
            pl.BlockSpec((tm, gw), row),
            pl.BlockSpec((fw + gw, d), const, pipeline_mode=pl.Buffered(1)),
            pl.BlockSpec((1, d), const),
            pl.BlockSpec((d, tf), lambda i, j: (0, j)),
            pl.BlockSpec((d, tf), lambda i, j: (0, j)),
            pl.BlockSpec((tf, d), lambda i, j: (j, 0)),
            pl.BlockSpec((1, d), const),
        ],
        out_specs=pl.BlockSpec((tm, d), row),
        out_shape=jax.ShapeDtypeStruct((t, d), F32),
        scratch_shapes=[pltpu.VMEM((tm, d), F32), pltpu.VMEM((tm, d), BF16), pltpu.VMEM((tm, d), F32)],
        compiler_params=pltpu.CompilerParams(
            dimension_semantics=("arbitrary", "arbitrary"), vmem_limit_bytes=VMEM_LIMIT_BYTES),
        name="out_ffn",
    )(x2, yf, yg, w_out, g_ffn, w_gate, w_up, w_down, g_fin)


def kernel(x, norm_mix, w_in, w_fourier, sgu_norm, w_spatial, b_spatial, w_out, norm_ffn,
           w_gate, w_up, w_down, norm_final):
    b, s, d = x.shape
    depth = w_in.shape[0]
    gw = sgu_norm.shape[1]
    nheads_g = w_spatial.shape[1]
    cs, ss = _dft_mats(s)
    cs = cs.astype(BF16)
    ss = ss.astype(BF16)

    assert depth == 1, "only the single-layer block is implemented"
    l = 0
    x2 = x.reshape(b * s, d)
    wcs = _fold_fourier_weights(w_fourier[l])
    b_full = jnp.repeat(jnp.transpose(b_spatial[l]), gw // nheads_g, axis=1)
    p, q, yg = _mix_in(
        x2, norm_mix[l][None, :], w_in[l].astype(BF16), wcs, sgu_norm[l][None, :],
        w_spatial[l].astype(BF16), b_full, tm=512)
    fw = p.shape[1]
    yf = _seq_dft(p.reshape(b, s, fw), q.reshape(b, s, fw), cs, ss, tm=512, tn=512)
    out = _out_ffn(
        x2, yf.reshape(b * s, fw), yg, w_out[l].astype(BF16), norm_ffn[l][None, :],
        w_gate[l].astype(BF16), w_up[l].astype(BF16), w_down[l].astype(BF16),
        norm_final[None, :], tm=512, tf=512)
    return out.reshape(b, s, d)
```

```python
import functools
import math

import jax
import jax.numpy as jnp
from jax import lax
from jax.experimental import pallas as pl
from jax.experimental.pallas import tpu as pltpu

EPS = 1e-6
HEAD = 128
CHUNK = 128
F32 = jnp.float32
BF16 = jnp.bfloat16

VMEM_LIMIT_BYTES = 60000 * 1024
MXU_DEPTH = 256
SEQ_OUTER = 16


def _dft_mats(n):
    idx = jnp.arange(n, dtype=jnp.int32)
    ks = (idx[:, None] * idx[None, :]) % n
    ang = ks.astype(F32) * (2.0 * math.pi / n)
    scale = 1.0 / math.sqrt(n)
    return jnp.cos(ang) * scale, jnp.sin(ang) * scale


def _rms(x, g):
    ms = jnp.mean(x * x, axis=-1, keepdims=True)
    return x * lax.rsqrt(ms + EPS) * g


def _gelu_tanh(x):
    c = math.sqrt(2.0 / math.pi)
    return 0.5 * x * (1.0 + jnp.tanh(c * (x + 0.044715 * (x * x * x))))


def _fold_kernel(cd_ref, sd_ref, w_ref, o_ref):
    w = w_ref[0]
    wc = jnp.dot(cd_ref[...], w, preferred_element_type=F32, precision=lax.Precision.HIGHEST)
    ws = jnp.dot(sd_ref[...], w, preferred_element_type=F32, precision=lax.Precision.HIGHEST)
    o_ref[0, :, :HEAD] = wc.astype(BF16)
    o_ref[0, :, HEAD:] = ws.astype(BF16)


def _fold_fourier_weights(w_fourier):
    nh = w_fourier.shape[0]
    cd, sd = _dft_mats(HEAD)
    return pl.pallas_call(
        _fold_kernel,
        grid=(nh,),
        in_specs=[pl.BlockSpec((HEAD, HEAD), lambda h: (0, 0)),
                  pl.BlockSpec((HEAD, HEAD), lambda h: (0, 0)),
                  pl.BlockSpec((1, HEAD, HEAD), lambda h: (h, 0, 0))],
        out_specs=pl.BlockSpec((1, HEAD, 2 * HEAD), lambda h: (h, 0, 0)),
        out_shape=jax.ShapeDtypeStruct((nh, HEAD, 2 * HEAD), BF16),
        name="fold_fourier_weights",
    )(cd, sd, w_fourier)


def _mix_in_kernel(x_ref, gmix_ref, win_ref, wcs_ref, gsgu_ref, wsp_ref, bsp_ref,
                   p_ref, q_ref, yg_ref, z_ref, *, fw, gw):
    tm = x_ref.shape[0]
    nheads_f = fw // HEAD
    nheads_g = gw // HEAD
    nchunk = tm // CHUNK

    h = _rms(x_ref[...], gmix_ref[...]).astype(BF16)
    z_ref[...] = jnp.dot(h, win_ref[...], preferred_element_type=F32)

    for hd in range(nheads_f):
        cols = slice(hd * HEAD, (hd + 1) * HEAD)
        pq = jnp.dot(z_ref[:, cols].astype(BF16), wcs_ref[hd], preferred_element_type=F32)
        p_ref[:, cols] = pq[:, :HEAD].astype(BF16)
        q_ref[:, cols] = pq[:, HEAD:].astype(BF16)

    v = _rms(_gelu_tanh(z_ref[:, fw + gw:]), gsgu_ref[...]).astype(BF16)
    for hd in range(nheads_g):
        cols = slice(hd * HEAD, (hd + 1) * HEAD)
        vcat = jnp.concatenate(
            [v[n * CHUNK:(n + 1) * CHUNK, cols] for n in range(nchunk)], axis=1)
        sv = jnp.dot(wsp_ref[hd], vcat, preferred_element_type=F32)
        bias = bsp_ref[:, cols]
        for n in range(nchunk):
            rows = slice(n * CHUNK, (n + 1) * CHUNK)
            u = _gelu_tanh(z_ref[rows, fw + hd * HEAD: fw + (hd + 1) * HEAD])
            yg_ref[rows, cols] = (u * (sv[:, n * HEAD:(n + 1) * HEAD] + bias)).astype(BF16)


def _mix_in(x2, g_mix, w_in, wcs, g_sgu, w_sp, b_full, *, tm):
    t, d = x2.shape
    nin = w_in.shape[1]
    gw = g_sgu.shape[1]
    fw = nin - 2 * gw
    const2 = lambda i: (0, 0)
    const3 = lambda i: (0, 0, 0)
    single = pl.Buffered(1)
    return pl.pallas_call(
        functools.partial(_mix_in_kernel, fw=fw, gw=gw),
        grid=(t // tm,),
        in_specs=[
            pl.BlockSpec((tm, d), lambda i: (i, 0)),
            pl.BlockSpec((1, d), const2),
            pl.BlockSpec((d, nin), const2, pipeline_mode=single),
            pl.BlockSpec(wcs.shape, const3),
            pl.BlockSpec((1, gw), const2),
            pl.BlockSpec(w_sp.shape, const3),
            pl.BlockSpec(b_full.shape, const2),
        ],
        out_specs=[
            pl.BlockSpec((tm, fw), lambda i: (i, 0)),
            pl.BlockSpec((tm, fw), lambda i: (i, 0)),
            pl.BlockSpec((tm, gw), lambda i: (i, 0)),
        ],
        out_shape=[
            jax.ShapeDtypeStruct((t, fw), BF16),
            jax.ShapeDtypeStruct((t, fw), BF16),
            jax.ShapeDtypeStruct((t, gw), BF16),
        ],
        scratch_shapes=[pltpu.VMEM((tm, nin), F32)],
        compiler_params=pltpu.CompilerParams(
            dimension_semantics=("arbitrary",), vmem_limit_bytes=VMEM_LIMIT_BYTES),
        name="mix_in",
    )(x2, g_mix, w_in, wcs, g_sgu, w_sp, b_full)


def _stage1_matrix(n1, pack):
    c, s = _dft_mats(n1)
    blk = jnp.concatenate([jnp.concatenate([c, -s], axis=1),
                           jnp.concatenate([-s, -c], axis=1)], axis=0)
    return jnp.kron(jnp.eye(pack, dtype=F32), blk).astype(BF16)


def _stage2_matrices(n1, n2):
    n = n1 * n2
    k1 = jnp.arange(n1, dtype=jnp.int32)[:, None, None]
    k2 = jnp.arange(n2, dtype=jnp.int32)[None, :, None]
    s2 = jnp.arange(n2, dtype=jnp.int32)[None, None, :]
    ang = ((s2 * (k1 + n1 * k2)) % n).astype(F32) * (2.0 * math.pi / n)
    g = jnp.concatenate([jnp.cos(ang), jnp.sin(ang)], axis=-1) * (1.0 / math.sqrt(n2))
    return g.astype(BF16)


def _dft1_kernel(f_ref, p_ref, q_ref, tr_ref, ti_ref, *, n1, pack):
    lanes = p_ref.shape[-1]
    lc = lanes // pack
    p = p_ref[0]
    q = q_ref[0]
    pieces = []
    for j in range(pack):
        pieces += [p[:, j * lc:(j + 1) * lc], q[:, j * lc:(j + 1) * lc]]
    d = jnp.concatenate(pieces, axis=0)
    res = jnp.dot(f_ref[...], d, preferred_element_type=F32)
    for j in range(pack):
        r0 = j * 2 * n1
        tr_ref[0, :, j * lc:(j + 1) * lc] = res[r0:r0 + n1].astype(BF16)
        ti_ref[0, :, j * lc:(j + 1) * lc] = res[r0 + n1:r0 + 2 * n1].astype(BF16)


def _dft2_kernel(g_ref, tr_ref, ti_ref, o_ref):
    kb = g_ref.shape[0]
    w = tr_ref.shape[-1]
    for t in range(kb):
        d = jnp.concatenate([tr_ref[0, t], ti_ref[0, t]], axis=0)
        o_ref[0, :, t * w:(t + 1) * w] = jnp.dot(
            g_ref[t], d, preferred_element_type=F32).astype(o_ref.dtype)


def _seq_dft(p, q, *, batch, n1, lanes, kb):
    t, w = p.shape
    s = t // batch
    n2 = s // n1
    pack = MXU_DEPTH // (2 * n1)
    f1 = _stage1_matrix(n1, pack)
    g2 = _stage2_matrices(n1, n2)
    row_lanes = n2 * w
    p3 = p.reshape(batch, n1, row_lanes)
    q3 = q.reshape(batch, n1, row_lanes)
    blk = pl.BlockSpec((1, n1, lanes), lambda b, j: (b, 0, j))
    tr, ti = pl.pallas_call(
        functools.partial(_dft1_kernel, n1=n1, pack=pack),
        grid=(batch, row_lanes // lanes),
        in_specs=[pl.BlockSpec(f1.shape, lambda b, j: (0, 0)), blk, blk],
        out_specs=[blk, blk],
        out_shape=[jax.ShapeDtypeStruct(p3.shape, BF16)] * 2,
        compiler_params=pltpu.CompilerParams(
            dimension_semantics=("arbitrary", "arbitrary"), vmem_limit_bytes=VMEM_LIMIT_BYTES),
        name="seq_dft_stage1",
    )(f1, p3, q3)
    t4 = pl.BlockSpec((1, kb, n2, w), lambda b, k: (b, k, 0, 0))
    yf = pl.pallas_call(
        _dft2_kernel,
        grid=(batch, n1 // kb),
        in_specs=[pl.BlockSpec((kb, n2, 2 * n2), lambda b, k: (k, 0, 0)), t4, t4],
        out_specs=pl.BlockSpec((1, n2, kb * w), lambda b, k: (b, 0, k)),
        out_shape=jax.ShapeDtypeStruct((batch, n2, n1 * w), BF16),
        compiler_params=pltpu.CompilerParams(
            dimension_semantics=("arbitrary", "arbitrary"), vmem_limit_bytes=VMEM_LIMIT_BYTES),
        name="seq_dft_stage2",
    )(g2, tr.reshape(batch, n1, n2, w), ti.reshape(batch, n1, n2, w))
    return yf.reshape(t, w)


def _out_ffn_kernel(x_ref, yf_ref, yg_ref, wo_ref, gffn_ref, wg_ref, wu_ref, wd_ref, gfin_ref,
                    o_ref, x1_ref, h2_ref, acc_ref):
    j = pl.program_id(1)
    fw = yf_ref.shape[1]

    @pl.when(j == 0)
    def _():
        x1 = x_ref[...]
        x1 = x1 + jnp.dot(yf_ref[...], wo_ref[:fw, :], preferred_element_type=F32)
        x1 = x1 + jnp.dot(yg_ref[...], wo_ref[fw:, :], preferred_element_type=F32)
        x1_ref[...] = x1
        h2_ref[...] = _rms(x1, gffn_ref[...]).astype(BF16)
        acc_ref[...] = jnp.zeros_like(acc_ref)

    h2 = h2_ref[...]
    g = jnp.dot(h2, wg_ref[...], preferred_element_type=F32)
    u = jnp.dot(h2, wu_ref[...], preferred_element_type=F32)
    t = (g * jax.nn.sigmoid(g) * u).astype(BF16)
    acc_ref[...] += jnp.dot(t, wd_ref[...], preferred_element_type=F32)

    @pl.when(j == pl.num_programs(1) - 1)
    def _():
        o_ref[...] = _rms(x1_ref[...] + acc_ref[...], gfin_ref[...])


def _out_ffn(x2, yf, yg, w_out, g_ffn, w_gate, w_up, w_down, g_fin, *, tm, tf):
    t, d = x2.shape
    fw = yf.shape[1]
    gw = yg.shape[1]
    dff = w_gate.shape[1]
    row = lambda i, j: (i, 0)
    const = lambda i, j: (0, 0)
    return pl.pallas_call(
        _out_ffn_kernel,
        grid=(t // tm, dff // tf),
        in_specs=[
            pl.BlockSpec((tm, d), row),
            pl.BlockSpec((tm, fw), row),
            pl.BlockSpec((tm, gw), row),
            pl.BlockSpec((fw + gw, d), const, pipeline_mode=pl.Buffered(1)),
            pl.BlockSpec((1, d), const),
            pl.BlockSpec((d, tf), lambda i, j: (0, j)),
            pl.BlockSpec((d, tf), lambda i, j: (0, j)),
            pl.BlockSpec((tf, d), lambda i, j: (j, 0)),
            pl.BlockSpec((1, d), const),
        ],
        out_specs=pl.BlockSpec((tm, d), row),
        out_shape=jax.ShapeDtypeStruct((t, d), F32),
        scratch_shapes=[pltpu.VMEM((tm, d), F32), pltpu.VMEM((tm, d), BF16), pltpu.VMEM((tm, d), F32)],
        compiler_params=pltpu.CompilerParams(
            dimension_semantics=("arbitrary", "arbitrary"), vmem_limit_bytes=VMEM_LIMIT_BYTES),
        name="out_ffn",
    )(x2, yf, yg, w_out, g_ffn, w_gate, w_up, w_down, g_fin)


def kernel(x, norm_mix, w_in, w_fourier, sgu_norm, w_spatial, b_spatial, w_out, norm_ffn,
           w_gate, w_up, w_down, norm_final):
    b, s, d = x.shape
    depth = w_in.shape[0]
    gw = sgu_norm.shape[1]
    nheads_g = w_spatial.shape[1]
    assert depth == 1, "only the single-layer block is implemented"
    l = 0
    x2 = x.reshape(b * s, d)
    wcs = _fold_fourier_weights(w_fourier[l])
    b_full = jnp.repeat(jnp.transpose(b_spatial[l]), gw // nheads_g, axis=1)
    p, q, yg = _mix_in(
        x2, norm_mix[l][None, :], w_in[l].astype(BF16), wcs, sgu_norm[l][None, :],
        w_spatial[l].astype(BF16), b_full, tm=512)
    yf = _seq_dft(p, q, batch=b, n1=SEQ_OUTER, lanes=32768, kb=4)
    out = _out_ffn(
        x2, yf, yg, w_out[l].astype(BF16), norm_ffn[l][None, :],
        w_gate[l].astype(BF16), w_up[l].astype(BF16), w_down[l].astype(BF16),
        norm_final[None, :], tm=512, tf=512)
    return out.reshape(b, s, d)
```

```python
import functools
import math

import jax
import jax.numpy as jnp
from jax import lax
from jax.experimental import pallas as pl
from jax.experimental.pallas import tpu as pltpu

EPS = 1e-6
HEAD = 128
CHUNK = 128
F32 = jnp.float32
BF16 = jnp.bfloat16

VMEM_LIMIT_BYTES = 60000 * 1024
LANES = 128
SEQ_OUTER = 16


def _dft_mats(n):
    idx = jnp.arange(n, dtype=jnp.int32)
    assert n & (n - 1) == 0
    ks = jnp.bitwise_and(idx[:, None] * idx[None, :], n - 1)
    ang = ks.astype(F32) * (2.0 * math.pi / n)
    scale = 1.0 / math.sqrt(n)
    return jnp.cos(ang) * scale, jnp.sin(ang) * scale


def _rms(x, g):
    ms = jnp.mean(x * x, axis=-1, keepdims=True)
    return x * lax.rsqrt(ms + EPS) * g


def _gelu_tanh(x):
    c = math.sqrt(2.0 / math.pi)
    return 0.5 * x * (1.0 + jnp.tanh(c * (x + 0.044715 * (x * x * x))))


def _fold_kernel(cd_ref, sd_ref, w_ref, o_ref):
    w = w_ref[0]
    wc = jnp.dot(cd_ref[...], w, preferred_element_type=F32, precision=lax.Precision.HIGHEST)
    ws = jnp.dot(sd_ref[...], w, preferred_element_type=F32, precision=lax.Precision.HIGHEST)
    o_ref[0, :, :HEAD] = wc.astype(BF16)
    o_ref[0, :, HEAD:] = ws.astype(BF16)


def _fold_fourier_weights(w_fourier):
    nh = w_fourier.shape[0]
    cd, sd = _dft_mats(HEAD)
    return pl.pallas_call(
        _fold_kernel,
        grid=(nh,),
        in_specs=[pl.BlockSpec((HEAD, HEAD), lambda h: (0, 0)),
                  pl.BlockSpec((HEAD, HEAD), lambda h: (0, 0)),
                  pl.BlockSpec((1, HEAD, HEAD), lambda h: (h, 0, 0))],
        out_specs=pl.BlockSpec((1, HEAD, 2 * HEAD), lambda h: (h, 0, 0)),
        out_shape=jax.ShapeDtypeStruct((nh, HEAD, 2 * HEAD), BF16),
        name="fold_fourier_weights",
    )(cd, sd, w_fourier)


def _mix_in_kernel(x_ref, gmix_ref, win_ref, wcs_ref, gsgu_ref, wsp_ref, bsp_ref,
                   p_ref, q_ref, yg_ref, z_ref, *, fw, gw):
    tm = x_ref.shape[0]
    nheads_f = fw // HEAD
    nheads_g = gw // HEAD
    nchunk = tm // CHUNK

    h = _rms(x_ref[...], gmix_ref[...]).astype(BF16)
    z_ref[...] = jnp.dot(h, win_ref[...], preferred_element_type=F32)

    for hd in range(nheads_f):
        cols = slice(hd * HEAD, (hd + 1) * HEAD)
        pq = jnp.dot(z_ref[:, cols].astype(BF16), wcs_ref[hd], preferred_element_type=F32)
        p_ref[:, cols] = pq[:, :HEAD].astype(BF16)
        q_ref[:, cols] = pq[:, HEAD:].astype(BF16)

    v = _rms(_gelu_tanh(z_ref[:, fw + gw:]), gsgu_ref[...]).astype(BF16)
    for hd in range(nheads_g):
        cols = slice(hd * HEAD, (hd + 1) * HEAD)
        vcat = jnp.concatenate(
            [v[n * CHUNK:(n + 1) * CHUNK, cols] for n in range(nchunk)], axis=1)
        sv = jnp.dot(wsp_ref[hd], vcat, preferred_element_type=F32)
        bias = bsp_ref[:, cols]
        for n in range(nchunk):
            rows = slice(n * CHUNK, (n + 1) * CHUNK)
            u = _gelu_tanh(z_ref[rows, fw + hd * HEAD: fw + (hd + 1) * HEAD])
            yg_ref[rows, cols] = (u * (sv[:, n * HEAD:(n + 1) * HEAD] + bias)).astype(BF16)


def _mix_in(x2, g_mix, w_in, wcs, g_sgu, w_sp, b_full, *, tm):
    t, d = x2.shape
    nin = w_in.shape[1]
    gw = g_sgu.shape[1]
    fw = nin - 2 * gw
    const2 = lambda i: (0, 0)
    const3 = lambda i: (0, 0, 0)
    single = pl.Buffered(1)
    return pl.pallas_call(
        functools.partial(_mix_in_kernel, fw=fw, gw=gw),
        grid=(t // tm,),
        in_specs=[
            pl.BlockSpec((tm, d), lambda i: (i, 0)),
            pl.BlockSpec((1, d), const2),
            pl.BlockSpec((d, nin), const2, pipeline_mode=single),
            pl.BlockSpec(wcs.shape, const3),
            pl.BlockSpec((1, gw), const2),
            pl.BlockSpec(w_sp.shape, const3),
            pl.BlockSpec(b_full.shape, const2),
        ],
        out_specs=[
            pl.BlockSpec((tm, fw), lambda i: (i, 0)),
            pl.BlockSpec((tm, fw), lambda i: (i, 0)),
            pl.BlockSpec((tm, gw), lambda i: (i, 0)),
        ],
        out_shape=[
            jax.ShapeDtypeStruct((t, fw), BF16),
            jax.ShapeDtypeStruct((t, fw), BF16),
            jax.ShapeDtypeStruct((t, gw), BF16),
        ],
        scratch_shapes=[pltpu.VMEM((tm, nin), F32)],
        compiler_params=pltpu.CompilerParams(
            dimension_semantics=("arbitrary",), vmem_limit_bytes=VMEM_LIMIT_BYTES),
        name="mix_in",
    )(x2, g_mix, w_in, wcs, g_sgu, w_sp, b_full)


def _stage2_matrices(n1, n2):
    n = n1 * n2
    assert n & (n - 1) == 0
    k1 = jnp.arange(n1, dtype=jnp.int32)[:, None, None]
    k2 = jnp.arange(n2, dtype=jnp.int32)[None, :, None]
    s2 = jnp.arange(n2, dtype=jnp.int32)[None, None, :]
    ang = jnp.bitwise_and(s2 * (k1 + n1 * k2), n - 1).astype(F32) * (2.0 * math.pi / n)
    g = jnp.concatenate([jnp.cos(ang), -jnp.sin(ang)], axis=-1) * (1.0 / math.sqrt(n))
    return g.astype(BF16)


def _fft_pos(xr, xi):
    n = len(xr)
    if n == 1:
        return xr, xi
    er, ei = _fft_pos(xr[0::2], xi[0::2])
    odr, odi = _fft_pos(xr[1::2], xi[1::2])
    outr, outi = [None] * n, [None] * n
    half = n // 2
    for k in range(half):
        a, b = odr[k], odi[k]
        if k == 0:
            tr, ti = a, b
        elif 4 * k == n:
            tr, ti = -b, a
        elif 8 * k == n:
            tr, ti = (a - b) * math.sqrt(0.5), (a + b) * math.sqrt(0.5)
        elif 8 * k == 3 * n:
            tr, ti = (a + b) * -math.sqrt(0.5), (a - b) * math.sqrt(0.5)
        else:
            c, s = math.cos(2.0 * math.pi * k / n), math.sin(2.0 * math.pi * k / n)
            tr, ti = a * c - b * s, b * c + a * s
        outr[k], outi[k] = er[k] + tr, ei[k] + ti
        outr[k + half], outi[k + half] = er[k] - tr, ei[k] - ti
    return outr, outi


def _seq_dft_kernel(g_ref, p_ref, q_ref, o_ref, t_ref, r_ref, *, n1, n2, pitch):
    tc = p_ref.shape[-1]
    nlt = tc // LANES
    rb = 16

    def stage1(i, carry):
        r0 = pl.multiple_of(i * rb, rb)
        for j in range(nlt):
            lanes = slice(j * LANES, (j + 1) * LANES)
            zr = [p_ref[0, pl.ds(s1 * n2 + r0, rb), lanes].astype(F32) for s1 in range(n1)]
            zi = [q_ref[0, pl.ds(s1 * n2 + r0, rb), lanes].astype(F32) for s1 in range(n1)]
            tr, ti = _fft_pos(zr, zi)
            for k1 in range(n1):
                t_ref[k1, pl.ds(r0, rb), lanes] = tr[k1].astype(BF16)
                t_ref[k1, pl.ds(n2 + r0, rb), lanes] = ti[k1].astype(BF16)
        return carry

    lax.fori_loop(0, n2 // rb, stage1, 0)

    for k1 in range(n1):
        res = jnp.dot(g_ref[k1], t_ref[k1], preferred_element_type=F32)
        for j in range(nlt):
            r_ref[j, k1 * pitch:k1 * pitch + n2, :] = res[:, j * LANES:(j + 1) * LANES]

    def interleave(k2, carry):
        row = pl.multiple_of(k2 * n1, n1)
        for j in range(nlt):
            blk = r_ref[j, pl.ds(k2, n1, stride=pitch), :]
            o_ref[0, pl.ds(row, n1), j * LANES:(j + 1) * LANES] = blk.astype(o_ref.dtype)
        return carry

    lax.fori_loop(0, n2, interleave, 0, unroll=8)


def _seq_dft(p, q, *, batch, n1, tc):
    t, w = p.shape
    s = t // batch
    n2 = s // n1
    pitch = n2 + 8
    g2 = _stage2_matrices(n1, n2)
    blk = pl.BlockSpec((1, s, tc), lambda b, j: (b, 0, j))
    yf = pl.pallas_call(
        functools.partial(_seq_dft_kernel, n1=n1, n2=n2, pitch=pitch),
        grid=(batch, w // tc),
        in_specs=[pl.BlockSpec(g2.shape, lambda b, j: (0, 0, 0), pipeline_mode=pl.Buffered(1)),
                  blk, blk],
        out_specs=blk,
        out_shape=jax.ShapeDtypeStruct((batch, s, w), BF16),
        scratch_shapes=[pltpu.VMEM((n1, 2 * n2, tc), BF16),
                        pltpu.VMEM((tc // LANES, n1 * pitch, LANES), F32)],
        compiler_params=pltpu.CompilerParams(
            dimension_semantics=("arbitrary", "arbitrary"), vmem_limit_bytes=VMEM_LIMIT_BYTES),
        name="seq_dft",
    )(g2, p.reshape(batch, s, w), q.reshape(batch, s, w))
    return yf.reshape(t, w)


def _out_ffn_kernel(x_ref, yf_ref, yg_ref, wo_ref, gffn_ref, wg_ref, wu_ref, wd_ref, gfin_ref,
                    o_ref, x1_ref, h2_ref, acc_ref):
    j = pl.program_id(1)
    fw = yf_ref.shape[1]

    @pl.when(j == 0)
    def _():
        x1 = x_ref[...]
        x1 = x1 + jnp.dot(yf_ref[...], wo_ref[:fw, :], preferred_element_type=F32)
        x1 = x1 + jnp.dot(yg_ref[...], wo_ref[fw:, :], preferred_element_type=F32)
        x1_ref[...] = x1
        h2_ref[...] = _rms(x1, gffn_ref[...]).astype(BF16)
        acc_ref[...] = jnp.zeros_like(acc_ref)

    h2 = h2_ref[...]
    g = jnp.dot(h2, wg_ref[...], preferred_element_type=F32)
    u = jnp.dot(h2, wu_ref[...], preferred_element_type=F32)
    t = (g * jax.nn.sigmoid(g) * u).astype(BF16)
    acc_ref[...] += jnp.dot(t, wd_ref[...], preferred_element_type=F32)

    @pl.when(j == pl.num_programs(1) - 1)
    def _():
        o_ref[...] = _rms(x1_ref[...] + acc_ref[...], gfin_ref[...])


def _out_ffn(x2, yf, yg, w_out, g_ffn, w_gate, w_up, w_down, g_fin, *, tm, tf):
    t, d = x2.shape
    fw = yf.shape[1]
    gw = yg.shape[1]
    dff = w_gate.shape[1]
    row = lambda i, j: (i, 0)
    const = lambda i, j: (0, 0)
    return pl.pallas_call(
        _out_ffn_kernel,
        grid=(t // tm, dff // tf),
        in_specs=[
            pl.BlockSpec((tm, d), row),
            pl.BlockSpec((tm, fw), row),
            pl.BlockSpec((tm, gw), row),
            pl.BlockSpec((fw + gw, d), const, pipeline_mode=pl.Buffered(1)),
            pl.BlockSpec((1, d), const),
            pl.BlockSpec((d, tf), lambda i, j: (0, j)),
            pl.BlockSpec((d, tf), lambda i, j: (0, j)),
            pl.BlockSpec((tf, d), lambda i, j: (j, 0)),
            pl.BlockSpec((1, d), const),
        ],
        out_specs=pl.BlockSpec((tm, d), row),
        out_shape=jax.ShapeDtypeStruct((t, d), F32),
        scratch_shapes=[pltpu.VMEM((tm, d), F32), pltpu.VMEM((tm, d), BF16), pltpu.VMEM((tm, d), F32)],
        compiler_params=pltpu.CompilerParams(
            dimension_semantics=("arbitrary", "arbitrary"), vmem_limit_bytes=VMEM_LIMIT_BYTES),
        name="out_ffn",
    )(x2, yf, yg, w_out, g_ffn, w_gate, w_up, w_down, g_fin)


def kernel(x, norm_mix, w_in, w_fourier, sgu_norm, w_spatial, b_spatial, w_out, norm_ffn,
           w_gate, w_up, w_down, norm_final):
    b, s, d = x.shape
    depth = w_in.shape[0]
    gw = sgu_norm.shape[1]
    nheads_g = w_spatial.shape[1]
    assert depth == 1, "only the single-layer block is implemented"
    l = 0
    x2 = x.reshape(b * s, d)
    wcs = _fold_fourier_weights(w_fourier[l])
    b_full = jnp.repeat(jnp.transpose(b_spatial[l]), gw // nheads_g, axis=1)
    p, q, yg = _mix_in(
        x2, norm_mix[l][None, :], w_in[l].astype(BF16), wcs, sgu_norm[l][None, :],
        w_spatial[l].astype(BF16), b_full, tm=512)
    yf = _seq_dft(p, q, batch=b, n1=SEQ_OUTER, tc=256)
    out = _out_ffn(
        x2, yf, yg, w_out[l].astype(BF16), norm_ffn[l][None, :],
        w_gate[l].astype(BF16), w_up[l].astype(BF16), w_down[l].astype(BF16),
        norm_final[None, :], tm=512, tf=512)
    return out.reshape(b, s, d)
```

```python
import functools
import math

import jax
import jax.numpy as jnp
from jax import lax
from jax.experimental import pallas as pl
from jax.experimental.pallas import tpu as pltpu

EPS = 1e-6
HEAD = 128
CHUNK = 128
F32 = jnp.float32
BF16 = jnp.bfloat16

VMEM_LIMIT_BYTES = 60000 * 1024
LANES = 128
SEQ_OUTER = 16


def _dft_mats(n):
    idx = jnp.arange(n, dtype=jnp.int32)
    assert n & (n - 1) == 0
    ks = jnp.bitwise_and(idx[:, None] * idx[None, :], n - 1)
    ang = ks.astype(F32) * (2.0 * math.pi / n)
    scale = 1.0 / math.sqrt(n)
    return jnp.cos(ang) * scale, jnp.sin(ang) * scale


def _rms(x, g):
    ms = jnp.mean(x * x, axis=-1, keepdims=True)
    return x * lax.rsqrt(ms + EPS) * g


def _gelu_tanh(x):
    c = math.sqrt(2.0 / math.pi)
    return 0.5 * x * (1.0 + jnp.tanh(c * (x + 0.044715 * (x * x * x))))


def _fold_kernel(cd_ref, sd_ref, w_ref, o_ref):
    w = w_ref[0]
    wc = jnp.dot(cd_ref[...], w, preferred_element_type=F32, precision=lax.Precision.HIGHEST)
    ws = jnp.dot(sd_ref[...], w, preferred_element_type=F32, precision=lax.Precision.HIGHEST)
    o_ref[0, :, :HEAD] = wc.astype(BF16)
    o_ref[0, :, HEAD:] = ws.astype(BF16)


def _fold_fourier_weights(w_fourier):
    nh = w_fourier.shape[0]
    cd, sd = _dft_mats(HEAD)
    return pl.pallas_call(
        _fold_kernel,
        grid=(nh,),
        in_specs=[pl.BlockSpec((HEAD, HEAD), lambda h: (0, 0)),
                  pl.BlockSpec((HEAD, HEAD), lambda h: (0, 0)),
                  pl.BlockSpec((1, HEAD, HEAD), lambda h: (h, 0, 0))],
        out_specs=pl.BlockSpec((1, HEAD, 2 * HEAD), lambda h: (h, 0, 0)),
        out_shape=jax.ShapeDtypeStruct((nh, HEAD, 2 * HEAD), BF16),
        name="fold_fourier_weights",
    )(cd, sd, w_fourier)


def _mix_in_kernel(x_ref, gmix_ref, win_ref, wcs_ref, gsgu_ref, wsp_ref, bsp_ref,
                   p_ref, q_ref, yg_ref, z_ref, *, fw, gw):
    tm = x_ref.shape[0]
    nheads_f = fw // HEAD
    nheads_g = gw // HEAD
    nchunk = tm // CHUNK

    h = _rms(x_ref[...], gmix_ref[...]).astype(BF16)
    z_ref[...] = jnp.dot(h, win_ref[...], preferred_element_type=F32)

    for hd in range(nheads_f):
        cols = slice(hd * HEAD, (hd + 1) * HEAD)
        pq = jnp.dot(z_ref[:, cols].astype(BF16), wcs_ref[hd], preferred_element_type=F32)
        p_ref[:, cols] = pq[:, :HEAD].astype(BF16)
        q_ref[:, cols] = pq[:, HEAD:].astype(BF16)

    v = _rms(_gelu_tanh(z_ref[:, fw + gw:]), gsgu_ref[...]).astype(BF16)
    for hd in range(nheads_g):
        cols = slice(hd * HEAD, (hd + 1) * HEAD)
        vcat = jnp.concatenate(
            [v[n * CHUNK:(n + 1) * CHUNK, cols] for n in range(nchunk)], axis=1)
        sv = jnp.dot(wsp_ref[hd], vcat, preferred_element_type=F32)
        bias = bsp_ref[:, cols]
        for n in range(nchunk):
            rows = slice(n * CHUNK, (n + 1) * CHUNK)
            u = _gelu_tanh(z_ref[rows, fw + hd * HEAD: fw + (hd + 1) * HEAD])
            yg_ref[rows, cols] = (u * (sv[:, n * HEAD:(n + 1) * HEAD] + bias)).astype(BF16)


def _mix_in(x2, g_mix, w_in, wcs, g_sgu, w_sp, b_full, *, tm):
    t, d = x2.shape
    nin = w_in.shape[1]
    gw = g_sgu.shape[1]
    fw = nin - 2 * gw
    const2 = lambda i: (0, 0)
    const3 = lambda i: (0, 0, 0)
    single = pl.Buffered(1)
    return pl.pallas_call(
        functools.partial(_mix_in_kernel, fw=fw, gw=gw),
        grid=(t // tm,),
        in_specs=[
            pl.BlockSpec((tm, d), lambda i: (i, 0)),
            pl.BlockSpec((1, d), const2),
            pl.BlockSpec((d, nin), const2, pipeline_mode=single),
            pl.BlockSpec(wcs.shape, const3),
            pl.BlockSpec((1, gw), const2),
            pl.BlockSpec(w_sp.shape, const3),
            pl.BlockSpec(b_full.shape, const2),
        ],
        out_specs=[
            pl.BlockSpec((tm, fw), lambda i: (i, 0)),
            pl.BlockSpec((tm, fw), lambda i: (i, 0)),
            pl.BlockSpec((tm, gw), lambda i: (i, 0)),
        ],
        out_shape=[
            jax.ShapeDtypeStruct((t, fw), BF16),
            jax.ShapeDtypeStruct((t, fw), BF16),
            jax.ShapeDtypeStruct((t, gw), BF16),
        ],
        scratch_shapes=[pltpu.VMEM((tm, nin), F32)],
        compiler_params=pltpu.CompilerParams(
            dimension_semantics=("arbitrary",), vmem_limit_bytes=VMEM_LIMIT_BYTES),
        name="mix_in",
    )(x2, g_mix, w_in, wcs, g_sgu, w_sp, b_full)


def _stage2_matrices(n1, n2):
    n = n1 * n2
    assert n & (n - 1) == 0
    k1 = jnp.arange(n1, dtype=jnp.int32)[:, None, None]
    k2 = jnp.arange(n2, dtype=jnp.int32)[None, :, None]
    s2 = jnp.arange(n2, dtype=jnp.int32)[None, None, :]
    ang = jnp.bitwise_and(s2 * (k1 + n1 * k2), n - 1).astype(F32) * (2.0 * math.pi / n)
    g = jnp.concatenate([jnp.cos(ang), -jnp.sin(ang)], axis=-1) * (1.0 / math.sqrt(n))
    return g.astype(BF16)


def _fft_pos(xr, xi):
    n = len(xr)
    if n == 1:
        return xr, xi
    er, ei = _fft_pos(xr[0::2], xi[0::2])
    odr, odi = _fft_pos(xr[1::2], xi[1::2])
    outr, outi = [None] * n, [None] * n
    half = n // 2
    for k in range(half):
        a, b = odr[k], odi[k]
        if k == 0:
            tr, ti = a, b
        elif 4 * k == n:
            tr, ti = -b, a
        elif 8 * k == n:
            tr, ti = (a - b) * math.sqrt(0.5), (a + b) * math.sqrt(0.5)
        elif 8 * k == 3 * n:
            tr, ti = (a + b) * -math.sqrt(0.5), (a - b) * math.sqrt(0.5)
        else:
            c, s = math.cos(2.0 * math.pi * k / n), math.sin(2.0 * math.pi * k / n)
            tr, ti = a * c - b * s, b * c + a * s
        outr[k], outi[k] = er[k] + tr, ei[k] + ti
        outr[k + half], outi[k + half] = er[k] - tr, ei[k] - ti
    return outr, outi


def _seq_dft_kernel(g_ref, p_ref, q_ref, o_ref, t_ref, r_ref, *, n1, n2, pitch):
    tc = p_ref.shape[-1]
    nlt = tc // LANES
    rb = 16

    def stage1(i, carry):
        r0 = pl.multiple_of(i * rb, rb)
        for j in range(nlt):
            lanes = slice(j * LANES, (j + 1) * LANES)
            zr = [p_ref[0, pl.ds(s1 * n2 + r0, rb), lanes].astype(F32) for s1 in range(n1)]
            zi = [q_ref[0, pl.ds(s1 * n2 + r0, rb), lanes].astype(F32) for s1 in range(n1)]
            tr, ti = _fft_pos(zr, zi)
            for k1 in range(n1):
                t_ref[k1, pl.ds(r0, rb), lanes] = tr[k1].astype(BF16)
                t_ref[k1, pl.ds(n2 + r0, rb), lanes] = ti[k1].astype(BF16)
        return carry

    lax.fori_loop(0, n2 // rb, stage1, 0)

    for k1 in range(n1):
        res = jnp.dot(g_ref[k1], t_ref[k1], preferred_element_type=F32)
        for j in range(nlt):
            r_ref[j, k1 * pitch:k1 * pitch + n2, :] = res[:, j * LANES:(j + 1) * LANES]

    def interleave(k2, carry):
        row = pl.multiple_of(k2 * n1, n1)
        for j in range(nlt):
            blk = r_ref[j, pl.ds(k2, n1, stride=pitch), :]
            o_ref[0, pl.ds(row, n1), j * LANES:(j + 1) * LANES] = blk.astype(o_ref.dtype)
        return carry

    lax.fori_loop(0, n2, interleave, 0, unroll=8)


def _seq_dft(p, q, *, batch, n1, tc):
    t, w = p.shape
    s = t // batch
    n2 = s // n1
    pitch = n2 + 8
    g2 = _stage2_matrices(n1, n2)
    blk = pl.BlockSpec((1, s, tc), lambda b, j: (b, 0, j))
    yf = pl.pallas_call(
        functools.partial(_seq_dft_kernel, n1=n1, n2=n2, pitch=pitch),
        grid=(batch, w // tc),
        in_specs=[pl.BlockSpec(g2.shape, lambda b, j: (0, 0, 0), pipeline_mode=pl.Buffered(1)),
                  blk, blk],
        out_specs=blk,
        out_shape=jax.ShapeDtypeStruct((batch, s, w), BF16),
        scratch_shapes=[pltpu.VMEM((n1, 2 * n2, tc), BF16),
                        pltpu.VMEM((tc // LANES, n1 * pitch, LANES), F32)],
        compiler_params=pltpu.CompilerParams(
            dimension_semantics=("arbitrary", "arbitrary"), vmem_limit_bytes=VMEM_LIMIT_BYTES),
        name="seq_dft",
    )(g2, p.reshape(batch, s, w), q.reshape(batch, s, w))
    return yf.reshape(t, w)


def _out_proj_kernel(x_ref, yf_ref, yg_ref, wo_ref, gffn_ref, x1_ref, h2_ref):
    fw = yf_ref.shape[1]
    x1 = x_ref[...]
    x1 = x1 + jnp.dot(yf_ref[...], wo_ref[:fw, :], preferred_element_type=F32)
    x1 = x1 + jnp.dot(yg_ref[...], wo_ref[fw:, :], preferred_element_type=F32)
    x1_ref[...] = x1
    h2_ref[...] = _rms(x1, gffn_ref[...]).astype(BF16)


def _out_proj(x2, yf, yg, w_out, g_ffn, *, tm):
    t, d = x2.shape
    fw = yf.shape[1]
    gw = yg.shape[1]
    row = lambda i: (i, 0)
    const = lambda i: (0, 0)
    return pl.pallas_call(
        _out_proj_kernel,
        grid=(t // tm,),
        in_specs=[
            pl.BlockSpec((tm, d), row),
            pl.BlockSpec((tm, fw), row),
            pl.BlockSpec((tm, gw), row),
            pl.BlockSpec((fw + gw, d), const, pipeline_mode=pl.Buffered(1)),
            pl.BlockSpec((1, d), const),
        ],
        out_specs=[pl.BlockSpec((tm, d), row), pl.BlockSpec((tm, d), row)],
        out_shape=[jax.ShapeDtypeStruct((t, d), F32), jax.ShapeDtypeStruct((t, d), BF16)],
        compiler_params=pltpu.CompilerParams(
            dimension_semantics=("arbitrary",), vmem_limit_bytes=VMEM_LIMIT_BYTES),
        name="out_proj",
    )(x2, yf, yg, w_out, g_ffn)


def _ffn_kernel(x1_ref, h2_ref, wg_ref, wu_ref, wd_ref, gfin_ref, o_ref, *, sub):
    j = pl.program_id(1)
    tf = wg_ref.shape[1]

    @pl.when(j == 0)
    def _():
        o_ref[...] = x1_ref[...]

    h2 = h2_ref[...]
    for c in range(tf // sub):
        cols = slice(c * sub, (c + 1) * sub)
        g = jnp.dot(h2, wg_ref[:, cols], preferred_element_type=F32)
        u = jnp.dot(h2, wu_ref[:, cols], preferred_element_type=F32)
        t = (g * jax.nn.sigmoid(g) * u).astype(BF16)
        o_ref[...] += jnp.dot(t, wd_ref[cols, :], preferred_element_type=F32)

    @pl.when(j == pl.num_programs(1) - 1)
    def _():
        o_ref[...] = _rms(o_ref[...], gfin_ref[...])


def _ffn(x1, h2, w_gate, w_up, w_down, g_fin, *, tm, tf, sub):
    t, d = x1.shape
    dff = w_gate.shape[1]
    row = lambda i, j: (i, 0)
    return pl.pallas_call(
        functools.partial(_ffn_kernel, sub=sub),
        grid=(t // tm, dff // tf),
        in_specs=[
            pl.BlockSpec((tm, d), row, pipeline_mode=pl.Buffered(1)),
            pl.BlockSpec((tm, d), row),
            pl.BlockSpec((d, tf), lambda i, j: (0, j)),
            pl.BlockSpec((d, tf), lambda i, j: (0, j)),
            pl.BlockSpec((tf, d), lambda i, j: (j, 0)),
            pl.BlockSpec((1, d), lambda i, j: (0, 0)),
        ],
        out_specs=pl.BlockSpec((tm, d), row),
        out_shape=jax.ShapeDtypeStruct((t, d), F32),
        compiler_params=pltpu.CompilerParams(
            dimension_semantics=("arbitrary", "arbitrary"), vmem_limit_bytes=VMEM_LIMIT_BYTES),
        name="ffn",
    )(x1, h2, w_gate, w_up, w_down, g_fin)


def kernel(x, norm_mix, w_in, w_fourier, sgu_norm, w_spatial, b_spatial, w_out, norm_ffn,
           w_gate, w_up, w_down, norm_final):
    b, s, d = x.shape
    depth = w_in.shape[0]
    gw = sgu_norm.shape[1]
    nheads_g = w_spatial.shape[1]
    assert depth == 1, "only the single-layer block is implemented"
    l = 0
    x2 = x.reshape(b * s, d)
    wcs = _fold_fourier_weights(w_fourier[l])
    b_full = jnp.repeat(jnp.transpose(b_spatial[l]), gw // nheads_g, axis=1)
    p, q, yg = _mix_in(
        x2, norm_mix[l][None, :], w_in[l].astype(BF16), wcs, sgu_norm[l][None, :],
        w_spatial[l].astype(BF16), b_full, tm=512)
    yf = _seq_dft(p, q, batch=b, n1=SEQ_OUTER, tc=256)
    x1, h2 = _out_proj(x2, yf, yg, w_out[l].astype(BF16), norm_ffn[l][None, :], tm=512)
    out = _ffn(x1, h2, w_gate[l].astype(BF16), w_up[l].astype(BF16), w_down[l].astype(BF16),
               norm_final[None, :], tm=1024, tf=512, sub=512)
    return out.reshape(b, s, d)
```

```python
import functools
import math

import jax
import jax.numpy as jnp
from jax import lax
from jax.experimental import pallas as pl
from jax.experimental.pallas import tpu as pltpu

EPS = 1e-6
HEAD = 128
CHUNK = 128
F32 = jnp.float32
BF16 = jnp.bfloat16

VMEM_LIMIT_BYTES = 60000 * 1024
LANES = 128
SEQ_OUTER = 16


def _dft_mats(n):
    idx = jnp.arange(n, dtype=jnp.int32)
    assert n & (n - 1) == 0
    ks = jnp.bitwise_and(idx[:, None] * idx[None, :], n - 1)
    ang = ks.astype(F32) * (2.0 * math.pi / n)
    scale = 1.0 / math.sqrt(n)
    return jnp.cos(ang) * scale, jnp.sin(ang) * scale


def _rms(x, g):
    ms = jnp.mean(x * x, axis=-1, keepdims=True)
    return x * lax.rsqrt(ms + EPS) * g


def _gelu_tanh(x):
    c = math.sqrt(2.0 / math.pi)
    return 0.5 * x * (1.0 + jnp.tanh(c * (x + 0.044715 * (x * x * x))))


def _cast_job(w, nsteps, index_map):
    rows, cols = w.shape
    chunk = rows // nsteps
    assert chunk * nsteps == rows and chunk % 16 == 0
    spec = pl.BlockSpec((chunk, cols), index_map)
    return spec, jax.ShapeDtypeStruct(w.shape, BF16)


def _fold_kernel(cd_ref, sd_ref, w_ref, o_ref):
    for hd in range(w_ref.shape[0]):
        w = w_ref[hd]
        wc = jnp.dot(cd_ref[...], w, preferred_element_type=F32, precision=lax.Precision.HIGHEST)
        ws = jnp.dot(sd_ref[...], w, preferred_element_type=F32, precision=lax.Precision.HIGHEST)
        o_ref[hd, :, :HEAD] = wc.astype(BF16)
        o_ref[hd, :, HEAD:] = ws.astype(BF16)


def _fold_fourier_weights(w_fourier):
    nh = w_fourier.shape[0]
    cd, sd = _dft_mats(HEAD)
    return pl.pallas_call(
        _fold_kernel,
        out_shape=jax.ShapeDtypeStruct((nh, HEAD, 2 * HEAD), BF16),
        name="fold_fourier_weights",
    )(cd, sd, w_fourier)


def _mix_in_kernel(x_ref, gmix_ref, win_ref, wcs_ref, gsgu_ref, wsp_ref, bsp_ref, cast_ref,
                   p_ref, q_ref, yg_ref, cast_out_ref, z_ref, *, fw, gw):
    tm = x_ref.shape[0]
    cast_out_ref[...] = cast_ref[...].astype(BF16)
    nheads_f = fw // HEAD
    nheads_g = gw // HEAD
    nchunk = tm // CHUNK

    h = _rms(x_ref[...], gmix_ref[...]).astype(BF16)
    z_ref[...] = jnp.dot(h, win_ref[...], preferred_element_type=F32)

    for hd in range(nheads_f):
        cols = slice(hd * HEAD, (hd + 1) * HEAD)
        pq = jnp.dot(z_ref[:, cols].astype(BF16), wcs_ref[hd], preferred_element_type=F32)
        p_ref[:, cols] = pq[:, :HEAD].astype(BF16)
        q_ref[:, cols] = pq[:, HEAD:].astype(BF16)

    v = _rms(_gelu_tanh(z_ref[:, fw + gw:]), gsgu_ref[...]).astype(BF16)
    for hd in range(nheads_g):
        cols = slice(hd * HEAD, (hd + 1) * HEAD)
        vcat = jnp.concatenate(
            [v[n * CHUNK:(n + 1) * CHUNK, cols] for n in range(nchunk)], axis=1)
        sv = jnp.dot(wsp_ref[hd], vcat, preferred_element_type=F32)
        bias = bsp_ref[:, cols]
        for n in range(nchunk):
            rows = slice(n * CHUNK, (n + 1) * CHUNK)
            u = _gelu_tanh(z_ref[rows, fw + hd * HEAD: fw + (hd + 1) * HEAD])
            yg_ref[rows, cols] = (u * (sv[:, n * HEAD:(n + 1) * HEAD] + bias)).astype(BF16)


def _mix_in(x2, g_mix, w_in, wcs, g_sgu, w_sp, b_full, w_cast, *, tm):
    t, d = x2.shape
    cast_spec, cast_shape = _cast_job(w_cast, t // tm, lambda i: (i, 0))
    nin = w_in.shape[1]
    gw = g_sgu.shape[1]
    fw = nin - 2 * gw
    const2 = lambda i: (0, 0)
    const3 = lambda i: (0, 0, 0)
    single = pl.Buffered(1)
    return pl.pallas_call(
        functools.partial(_mix_in_kernel, fw=fw, gw=gw),
        grid=(t // tm,),
        in_specs=[
            pl.BlockSpec((tm, d), lambda i: (i, 0)),
            pl.BlockSpec((1, d), const2),
            pl.BlockSpec((d, nin), const2, pipeline_mode=single),
            pl.BlockSpec(wcs.shape, const3),
            pl.BlockSpec((1, gw), const2),
            pl.BlockSpec(w_sp.shape, const3),
            pl.BlockSpec(b_full.shape, const2),
            cast_spec,
        ],
        out_specs=[
            pl.BlockSpec((tm, fw), lambda i: (i, 0)),
            pl.BlockSpec((tm, fw), lambda i: (i, 0)),
            pl.BlockSpec((tm, gw), lambda i: (i, 0)),
            cast_spec,
        ],
        out_shape=[
            jax.ShapeDtypeStruct((t, fw), BF16),
            jax.ShapeDtypeStruct((t, fw), BF16),
            jax.ShapeDtypeStruct((t, gw), BF16),
            cast_shape,
        ],
        scratch_shapes=[pltpu.VMEM((tm, nin), F32)],
        compiler_params=pltpu.CompilerParams(
            dimension_semantics=("arbitrary",), vmem_limit_bytes=VMEM_LIMIT_BYTES),
        name="mix_in",
    )(x2, g_mix, w_in, wcs, g_sgu, w_sp, b_full, w_cast)


def _stage2_matrices(n1, n2):
    n = n1 * n2
    assert n & (n - 1) == 0
    k1 = jnp.arange(n1, dtype=jnp.int32)[:, None, None]
    k2 = jnp.arange(n2, dtype=jnp.int32)[None, :, None]
    s2 = jnp.arange(n2, dtype=jnp.int32)[None, None, :]
    alpha = jnp.bitwise_and(s2 * k1, n - 1).astype(F32) * (2.0 * math.pi / n)
    beta = jnp.bitwise_and(s2 * k2, n2 - 1).astype(F32) * (2.0 * math.pi / n2)
    ca, sa, cb, sb = jnp.cos(alpha), jnp.sin(alpha), jnp.cos(beta), jnp.sin(beta)
    g = jnp.concatenate([ca * cb - sa * sb, -(sa * cb + ca * sb)], axis=-1) * (1.0 / math.sqrt(n))
    return g.astype(BF16)


def _fft_pos(xr, xi):
    n = len(xr)
    if n == 1:
        return xr, xi
    er, ei = _fft_pos(xr[0::2], xi[0::2])
    odr, odi = _fft_pos(xr[1::2], xi[1::2])
    outr, outi = [None] * n, [None] * n
    half = n // 2
    for k in range(half):
        a, b = odr[k], odi[k]
        if k == 0:
            tr, ti = a, b
        elif 4 * k == n:
            tr, ti = -b, a
        elif 8 * k == n:
            tr, ti = (a - b) * math.sqrt(0.5), (a + b) * math.sqrt(0.5)
        elif 8 * k == 3 * n:
            tr, ti = (a + b) * -math.sqrt(0.5), (a - b) * math.sqrt(0.5)
        else:
            c, s = math.cos(2.0 * math.pi * k / n), math.sin(2.0 * math.pi * k / n)
            tr, ti = a * c - b * s, b * c + a * s
        outr[k], outi[k] = er[k] + tr, ei[k] + ti
        outr[k + half], outi[k + half] = er[k] - tr, ei[k] - ti
    return outr, outi


def _seq_dft_kernel(g_ref, p_ref, q_ref, cast_a_ref, cast_b_ref, o_ref, cast_a_out_ref, cast_b_out_ref,
                    t_ref, r_ref, *, n1, n2, pitch):
    cast_a_out_ref[...] = cast_a_ref[...].astype(BF16)
    cast_b_out_ref[...] = cast_b_ref[...].astype(BF16)
    tc = p_ref.shape[-1]
    nlt = tc // LANES
    rb = 16

    def stage1(i, carry):
        r0 = pl.multiple_of(i * rb, rb)
        for j in range(nlt):
            lanes = slice(j * LANES, (j + 1) * LANES)
            zr = [p_ref[0, pl.ds(s1 * n2 + r0, rb), lanes].astype(F32) for s1 in range(n1)]
            zi = [q_ref[0, pl.ds(s1 * n2 + r0, rb), lanes].astype(F32) for s1 in range(n1)]
            tr, ti = _fft_pos(zr, zi)
            for k1 in range(n1):
                t_ref[k1, pl.ds(r0, rb), lanes] = tr[k1].astype(BF16)
                t_ref[k1, pl.ds(n2 + r0, rb), lanes] = ti[k1].astype(BF16)
        return carry

    lax.fori_loop(0, n2 // rb, stage1, 0)

    for k1 in range(n1):
        res = jnp.dot(g_ref[k1], t_ref[k1], preferred_element_type=F32)
        for j in range(nlt):
            r_ref[j, k1 * pitch:k1 * pitch + n2, :] = res[:, j * LANES:(j + 1) * LANES]

    def interleave(k2, carry):
        row = pl.multiple_of(k2 * n1, n1)
        for j in range(nlt):
            blk = r_ref[j, pl.ds(k2, n1, stride=pitch), :]
            o_ref[0, pl.ds(row, n1), j * LANES:(j + 1) * LANES] = blk.astype(o_ref.dtype)
        return carry

    lax.fori_loop(0, n2, interleave, 0, unroll=8)


def _seq_dft(p, q, w_cast_a, w_cast_b, *, batch, n1, tc):
    t, w = p.shape
    nj = w // tc
    step = lambda b, j: (b * nj + j, 0)
    cast_a_spec, cast_a_shape = _cast_job(w_cast_a, batch * nj, step)
    cast_b_spec, cast_b_shape = _cast_job(w_cast_b, batch * nj, step)
    s = t // batch
    n2 = s // n1
    pitch = n2 + 8
    g2 = _stage2_matrices(n1, n2)
    blk = pl.BlockSpec((1, s, tc), lambda b, j: (b, 0, j))
    yf, cast_a, cast_b = pl.pallas_call(
        functools.partial(_seq_dft_kernel, n1=n1, n2=n2, pitch=pitch),
        grid=(batch, nj),
        in_specs=[pl.BlockSpec(g2.shape, lambda b, j: (0, 0, 0), pipeline_mode=pl.Buffered(1)),
                  blk, blk, cast_a_spec, cast_b_spec],
        out_specs=[blk, cast_a_spec, cast_b_spec],
        out_shape=[jax.ShapeDtypeStruct((batch, s, w), BF16), cast_a_shape, cast_b_shape],
        scratch_shapes=[pltpu.VMEM((n1, 2 * n2, tc), BF16),
                        pltpu.VMEM((tc // LANES, n1 * pitch, LANES), F32)],
        compiler_params=pltpu.CompilerParams(
            dimension_semantics=("arbitrary", "arbitrary"), vmem_limit_bytes=VMEM_LIMIT_BYTES),
        name="seq_dft",
    )(g2, p.reshape(batch, s, w), q.reshape(batch, s, w), w_cast_a, w_cast_b)
    return yf.reshape(t, w), cast_a, cast_b


def _out_proj_kernel(x_ref, yf_ref, yg_ref, wo_ref, gffn_ref, cast_ref, x1_ref, h2_ref, cast_out_ref):
    cast_out_ref[...] = cast_ref[...].astype(BF16)
    fw = yf_ref.shape[1]
    x1 = x_ref[...]
    x1 = x1 + jnp.dot(yf_ref[...], wo_ref[:fw, :], preferred_element_type=F32)
    x1 = x1 + jnp.dot(yg_ref[...], wo_ref[fw:, :], preferred_element_type=F32)
    x1_ref[...] = x1
    h2_ref[...] = _rms(x1, gffn_ref[...]).astype(BF16)


def _out_proj(x2, yf, yg, w_out, g_ffn, w_cast, *, tm):
    t, d = x2.shape
    cast_spec, cast_shape = _cast_job(w_cast, t // tm, lambda i: (i, 0))
    fw = yf.shape[1]
    gw = yg.shape[1]
    row = lambda i: (i, 0)
    const = lambda i: (0, 0)
    return pl.pallas_call(
        _out_proj_kernel,
        grid=(t // tm,),
        in_specs=[
            pl.BlockSpec((tm, d), row),
            pl.BlockSpec((tm, fw), row),
            pl.BlockSpec((tm, gw), row),
            pl.BlockSpec((fw + gw, d), const, pipeline_mode=pl.Buffered(1)),
            pl.BlockSpec((1, d), const),
            cast_spec,
        ],
        out_specs=[pl.BlockSpec((tm, d), row), pl.BlockSpec((tm, d), row), cast_spec],
        out_shape=[jax.ShapeDtypeStruct((t, d), F32), jax.ShapeDtypeStruct((t, d), BF16), cast_shape],
        compiler_params=pltpu.CompilerParams(
            dimension_semantics=("arbitrary",), vmem_limit_bytes=VMEM_LIMIT_BYTES),
        name="out_proj",
    )(x2, yf, yg, w_out, g_ffn, w_cast)


def _ffn_kernel(x1_ref, h2_ref, wg_ref, wu_ref, wd_ref, gfin_ref, o_ref, *, sub):
    j = pl.program_id(1)
    tf = wg_ref.shape[1]

    @pl.when(j == 0)
    def _():
        o_ref[...] = x1_ref[...]

    h2 = h2_ref[...]
    for c in range(tf // sub):
        cols = slice(c * sub, (c + 1) * sub)
        g = jnp.dot(h2, wg_ref[:, cols], preferred_element_type=F32)
        u = jnp.dot(h2, wu_ref[:, cols], preferred_element_type=F32)
        t = (g * jax.nn.sigmoid(g) * u).astype(BF16)
        o_ref[...] += jnp.dot(t, wd_ref[cols, :], preferred_element_type=F32)

    @pl.when(j == pl.num_programs(1) - 1)
    def _():
        o_ref[...] = _rms(o_ref[...], gfin_ref[...])


def _ffn(x1, h2, w_gate, w_up, w_down, g_fin, *, tm, tf, sub):
    t, d = x1.shape
    dff = w_gate.shape[1]
    row = lambda i, j: (i, 0)
    return pl.pallas_call(
        functools.partial(_ffn_kernel, sub=sub),
        grid=(t // tm, dff // tf),
        in_specs=[
            pl.BlockSpec((tm, d), row, pipeline_mode=pl.Buffered(1)),
            pl.BlockSpec((tm, d), row),
            pl.BlockSpec((d, tf), lambda i, j: (0, j)),
            pl.BlockSpec((d, tf), lambda i, j: (0, j)),
            pl.BlockSpec((tf, d), lambda i, j: (j, 0)),
            pl.BlockSpec((1, d), lambda i, j: (0, 0)),
        ],
        out_specs=pl.BlockSpec((tm, d), row),
        out_shape=jax.ShapeDtypeStruct((t, d), F32),
        compiler_params=pltpu.CompilerParams(
            dimension_semantics=("arbitrary", "arbitrary"), vmem_limit_bytes=VMEM_LIMIT_BYTES),
        name="ffn",
    )(x1, h2, w_gate, w_up, w_down, g_fin)


def kernel(x, norm_mix, w_in, w_fourier, sgu_norm, w_spatial, b_spatial, w_out, norm_ffn,
           w_gate, w_up, w_down, norm_final):
    b, s, d = x.shape
    depth = w_in.shape[0]
    gw = sgu_norm.shape[1]
    nheads_g = w_spatial.shape[1]
    assert depth == 1, "only the single-layer block is implemented"
    l = 0
    x2 = x.reshape(b * s, d)
    wcs = _fold_fourier_weights(w_fourier[l])
    b_full = jnp.repeat(jnp.transpose(b_spatial[l]), gw // nheads_g, axis=1)
    p, q, yg, wg_bf = _mix_in(
        x2, norm_mix[l][None, :], w_in[l].astype(BF16), wcs, sgu_norm[l][None, :],
        w_spatial[l].astype(BF16), b_full, w_gate[l], tm=512)
    yf, wu_bf, wo_bf = _seq_dft(p, q, w_up[l], w_out[l], batch=b, n1=SEQ_OUTER, tc=256)
    x1, h2, wd_bf = _out_proj(x2, yf, yg, wo_bf, norm_ffn[l][None, :], w_down[l], tm=512)
    out = _ffn(x1, h2, wg_bf, wu_bf, wd_bf, norm_final[None, :], tm=1024, tf=512, sub=512)
    return out.reshape(b, s, d)
```

```python
import functools
import math

import jax
import jax.numpy as jnp
from jax import lax
from jax.experimental import pallas as pl
from jax.experimental.pallas import tpu as pltpu

EPS = 1e-6
HEAD = 128
CHUNK = 128
F32 = jnp.float32
BF16 = jnp.bfloat16

VMEM_LIMIT_BYTES = 60000 * 1024
LANES = 128
SEQ_OUTER = 16


def _dft_mats(n):
    idx = jnp.arange(n, dtype=jnp.int32)
    assert n & (n - 1) == 0
    ks = jnp.bitwise_and(idx[:, None] * idx[None, :], n - 1)
    ang = ks.astype(F32) * (2.0 * math.pi / n)
    scale = 1.0 / math.sqrt(n)
    return jnp.cos(ang) * scale, jnp.sin(ang) * scale


def _rms(x, g):
    ms = jnp.mean(x * x, axis=-1, keepdims=True)
    return x * lax.rsqrt(ms + EPS) * g


def _gelu_tanh(x):
    c = math.sqrt(2.0 / math.pi)
    return 0.5 * x * (1.0 + jnp.tanh(c * (x + 0.044715 * (x * x * x))))


def _cast_job(w, nsteps, index_map):
    rows, cols = w.shape
    chunk = rows // nsteps
    assert chunk * nsteps == rows and chunk % 16 == 0
    spec = pl.BlockSpec((chunk, cols), index_map)
    return spec, jax.ShapeDtypeStruct(w.shape, BF16)


def _fold_kernel(cd_ref, sd_ref, w_ref, o_ref):
    for hd in range(w_ref.shape[0]):
        w = w_ref[hd]
        wc = jnp.dot(cd_ref[...], w, preferred_element_type=F32, precision=lax.Precision.HIGHEST)
        ws = jnp.dot(sd_ref[...], w, preferred_element_type=F32, precision=lax.Precision.HIGHEST)
        o_ref[hd, :, :HEAD] = wc.astype(BF16)
        o_ref[hd, :, HEAD:] = ws.astype(BF16)


def _fold_fourier_weights(w_fourier):
    nh = w_fourier.shape[0]
    cd, sd = _dft_mats(HEAD)
    return pl.pallas_call(
        _fold_kernel,
        out_shape=jax.ShapeDtypeStruct((nh, HEAD, 2 * HEAD), BF16),
        name="fold_fourier_weights",
    )(cd, sd, w_fourier)


def _mix_in_kernel(x_ref, gmix_ref, win_ref, wcs_ref, gsgu_ref, wsp_ref, bsp_ref, cast_ref,
                   p_ref, q_ref, yg_ref, cast_out_ref, z_ref, *, fw, gw):
    tm = x_ref.shape[0]
    cast_out_ref[...] = cast_ref[...].astype(BF16)
    nheads_f = fw // HEAD
    nheads_g = gw // HEAD
    nchunk = tm // CHUNK

    h = _rms(x_ref[...], gmix_ref[...]).astype(BF16)
    z_ref[...] = jnp.dot(h, win_ref[...], preferred_element_type=F32)

    for hd in range(nheads_f):
        cols = slice(hd * HEAD, (hd + 1) * HEAD)
        pq = jnp.dot(z_ref[:, cols].astype(BF16), wcs_ref[hd], preferred_element_type=F32)
        p_ref[:, cols] = pq[:, :HEAD].astype(BF16)
        q_ref[:, cols] = pq[:, HEAD:].astype(BF16)

    v = _rms(_gelu_tanh(z_ref[:, fw + gw:]), gsgu_ref[...]).astype(BF16)
    for hd in range(nheads_g):
        cols = slice(hd * HEAD, (hd + 1) * HEAD)
        vcat = jnp.concatenate(
            [v[n * CHUNK:(n + 1) * CHUNK, cols] for n in range(nchunk)], axis=1)
        sv = jnp.dot(wsp_ref[hd], vcat, preferred_element_type=F32)
        bias = bsp_ref[:, cols]
        for n in range(nchunk):
            rows = slice(n * CHUNK, (n + 1) * CHUNK)
            u = _gelu_tanh(z_ref[rows, fw + hd * HEAD: fw + (hd + 1) * HEAD])
            yg_ref[rows, cols] = (u * (sv[:, n * HEAD:(n + 1) * HEAD] + bias)).astype(BF16)


def _mix_in(x2, g_mix, w_in, wcs, g_sgu, w_sp, b_full, w_cast, *, tm):
    t, d = x2.shape
    cast_spec, cast_shape = _cast_job(w_cast, t // tm, lambda i: (i, 0))
    nin = w_in.shape[1]
    gw = g_sgu.shape[1]
    fw = nin - 2 * gw
    const2 = lambda i: (0, 0)
    const3 = lambda i: (0, 0, 0)
    single = pl.Buffered(1)
    return pl.pallas_call(
        functools.partial(_mix_in_kernel, fw=fw, gw=gw),
        grid=(t // tm,),
        in_specs=[
            pl.BlockSpec((tm, d), lambda i: (i, 0)),
            pl.BlockSpec((1, d), const2),
            pl.BlockSpec((d, nin), const2, pipeline_mode=single),
            pl.BlockSpec(wcs.shape, const3),
            pl.BlockSpec((1, gw), const2),
            pl.BlockSpec(w_sp.shape, const3),
            pl.BlockSpec(b_full.shape, const2),
            cast_spec,
        ],
        out_specs=[
            pl.BlockSpec((tm, fw), lambda i: (i, 0)),
            pl.BlockSpec((tm, fw), lambda i: (i, 0)),
            pl.BlockSpec((tm, gw), lambda i: (i, 0)),
            cast_spec,
        ],
        out_shape=[
            jax.ShapeDtypeStruct((t, fw), BF16),
            jax.ShapeDtypeStruct((t, fw), BF16),
            jax.ShapeDtypeStruct((t, gw), BF16),
            cast_shape,
        ],
        scratch_shapes=[pltpu.VMEM((tm, nin), F32)],
        compiler_params=pltpu.CompilerParams(
            dimension_semantics=("arbitrary",), vmem_limit_bytes=VMEM_LIMIT_BYTES),
        name="mix_in",
    )(x2, g_mix, w_in, wcs, g_sgu, w_sp, b_full, w_cast)


def _stage2_matrices(n1, n2):
    n = n1 * n2
    assert n & (n - 1) == 0
    k1 = jnp.arange(n1, dtype=jnp.int32)[:, None, None]
    k2 = jnp.arange(n2, dtype=jnp.int32)[None, :, None]
    s2 = jnp.arange(n2, dtype=jnp.int32)[None, None, :]
    alpha = jnp.bitwise_and(s2 * k1, n - 1).astype(F32) * (2.0 * math.pi / n)
    beta = jnp.bitwise_and(s2 * k2, n2 - 1).astype(F32) * (2.0 * math.pi / n2)
    ca, sa, cb, sb = jnp.cos(alpha), jnp.sin(alpha), jnp.cos(beta), jnp.sin(beta)
    g = jnp.concatenate([ca * cb - sa * sb, -(sa * cb + ca * sb)], axis=-1) * (1.0 / math.sqrt(n))
    return g.astype(BF16)


def _fft_pos(xr, xi):
    n = len(xr)
    if n == 1:
        return xr, xi
    er, ei = _fft_pos(xr[0::2], xi[0::2])
    odr, odi = _fft_pos(xr[1::2], xi[1::2])
    outr, outi = [None] * n, [None] * n
    half = n // 2
    for k in range(half):
        a, b = odr[k], odi[k]
        if k == 0:
            tr, ti = a, b
        elif 4 * k == n:
            tr, ti = -b, a
        elif 8 * k == n:
            tr, ti = (a - b) * math.sqrt(0.5), (a + b) * math.sqrt(0.5)
        elif 8 * k == 3 * n:
            tr, ti = (a + b) * -math.sqrt(0.5), (a - b) * math.sqrt(0.5)
        else:
            c, s = math.cos(2.0 * math.pi * k / n), math.sin(2.0 * math.pi * k / n)
            tr, ti = a * c - b * s, b * c + a * s
        outr[k], outi[k] = er[k] + tr, ei[k] + ti
        outr[k + half], outi[k + half] = er[k] - tr, ei[k] - ti
    return outr, outi


def _seq_dft_kernel(g_ref, p_ref, q_ref, cast_a_ref, cast_b_ref, o_ref, cast_a_out_ref, cast_b_out_ref,
                    t_ref, r_ref, *, n1, n2, pitch):
    cast_a_out_ref[...] = cast_a_ref[...].astype(BF16)
    cast_b_out_ref[...] = cast_b_ref[...].astype(BF16)
    tc = p_ref.shape[-1]
    nlt = tc // LANES
    rb = 16

    def stage1(i, carry):
        r0 = pl.multiple_of(i * rb, rb)
        for j in range(nlt):
            lanes = slice(j * LANES, (j + 1) * LANES)
            zr = [p_ref[0, pl.ds(s1 * n2 + r0, rb), lanes].astype(F32) for s1 in range(n1)]
            zi = [q_ref[0, pl.ds(s1 * n2 + r0, rb), lanes].astype(F32) for s1 in range(n1)]
            tr, ti = _fft_pos(zr, zi)
            for k1 in range(n1):
                t_ref[k1, pl.ds(r0, rb), lanes] = tr[k1].astype(BF16)
                t_ref[k1, pl.ds(n2 + r0, rb), lanes] = ti[k1].astype(BF16)
        return carry

    lax.fori_loop(0, n2 // rb, stage1, 0)

    for k1 in range(n1):
        res = jnp.dot(g_ref[k1], t_ref[k1], preferred_element_type=F32)
        for j in range(nlt):
            r_ref[j, k1 * pitch:k1 * pitch + n2, :] = res[:, j * LANES:(j + 1) * LANES]

    def interleave(k2, carry):
        row = pl.multiple_of(k2 * n1, n1)
        for j in range(nlt):
            blk = r_ref[j, pl.ds(k2, n1, stride=pitch), :]
            o_ref[0, pl.ds(row, n1), j * LANES:(j + 1) * LANES] = blk.astype(o_ref.dtype)
        return carry

    lax.fori_loop(0, n2, interleave, 0, unroll=8)


def _seq_dft(p, q, w_cast_a, w_cast_b, *, batch, n1, tc):
    t, w = p.shape
    nj = w // tc
    step = lambda b, j: (b * nj + j, 0)
    cast_a_spec, cast_a_shape = _cast_job(w_cast_a, batch * nj, step)
    cast_b_spec, cast_b_shape = _cast_job(w_cast_b, batch * nj, step)
    s = t // batch
    n2 = s // n1
    pitch = n2 + 8
    g2 = _stage2_matrices(n1, n2)
    blk = pl.BlockSpec((1, s, tc), lambda b, j: (b, 0, j))
    yf, cast_a, cast_b = pl.pallas_call(
        functools.partial(_seq_dft_kernel, n1=n1, n2=n2, pitch=pitch),
        grid=(batch, nj),
        in_specs=[pl.BlockSpec(g2.shape, lambda b, j: (0, 0, 0), pipeline_mode=pl.Buffered(1)),
                  blk, blk, cast_a_spec, cast_b_spec],
        out_specs=[blk, cast_a_spec, cast_b_spec],
        out_shape=[jax.ShapeDtypeStruct((batch, s, w), BF16), cast_a_shape, cast_b_shape],
        scratch_shapes=[pltpu.VMEM((n1, 2 * n2, tc), BF16),
                        pltpu.VMEM((tc // LANES, n1 * pitch, LANES), F32)],
        compiler_params=pltpu.CompilerParams(
            dimension_semantics=("arbitrary", "arbitrary"), vmem_limit_bytes=VMEM_LIMIT_BYTES),
        name="seq_dft",
    )(g2, p.reshape(batch, s, w), q.reshape(batch, s, w), w_cast_a, w_cast_b)
    return yf.reshape(t, w), cast_a, cast_b


def _out_proj_kernel(x_ref, yf_ref, yg_ref, wo_ref, gffn_ref, cast_ref, x1_ref, h2_ref, cast_out_ref):
    cast_out_ref[...] = cast_ref[...].astype(BF16)
    fw = yf_ref.shape[1]
    x1 = x_ref[...]
    x1 = x1 + jnp.dot(yf_ref[...], wo_ref[:fw, :], preferred_element_type=F32)
    x1 = x1 + jnp.dot(yg_ref[...], wo_ref[fw:, :], preferred_element_type=F32)
    x1_ref[...] = x1
    h2_ref[...] = _rms(x1, gffn_ref[...]).astype(BF16)


def _out_proj(x2, yf, yg, w_out, g_ffn, w_cast, *, tm):
    t, d = x2.shape
    cast_spec, cast_shape = _cast_job(w_cast, t // tm, lambda i: (i, 0))
    fw = yf.shape[1]
    gw = yg.shape[1]
    row = lambda i: (i, 0)
    const = lambda i: (0, 0)
    return pl.pallas_call(
        _out_proj_kernel,
        grid=(t // tm,),
        in_specs=[
            pl.BlockSpec((tm, d), row),
            pl.BlockSpec((tm, fw), row),
            pl.BlockSpec((tm, gw), row),
            pl.BlockSpec((fw + gw, d), const, pipeline_mode=pl.Buffered(1)),
            pl.BlockSpec((1, d), const),
            cast_spec,
        ],
        out_specs=[pl.BlockSpec((tm, d), row), pl.BlockSpec((tm, d), row), cast_spec],
        out_shape=[jax.ShapeDtypeStruct((t, d), F32), jax.ShapeDtypeStruct((t, d), BF16), cast_shape],
        compiler_params=pltpu.CompilerParams(
            dimension_semantics=("arbitrary",), vmem_limit_bytes=VMEM_LIMIT_BYTES),
        name="out_proj",
    )(x2, yf, yg, w_out, g_ffn, w_cast)


def _ffn_kernel(x1_hbm, h2_ref, wg_ref, wu_ref, wd_ref, gfin_ref, o_ref, t_ref, sem):
    i = pl.program_id(0)
    j = pl.program_id(1)
    nj = pl.num_programs(1) - 1
    tm = o_ref.shape[0]
    slot = j % 2
    x1_copy = pltpu.make_async_copy(x1_hbm.at[pl.ds(i * tm, tm), :], o_ref, sem)

    def gated():
        h2 = h2_ref[...]
        g = jnp.dot(h2, wg_ref[...], preferred_element_type=F32)
        u = jnp.dot(h2, wu_ref[...], preferred_element_type=F32)
        return (g * jax.nn.sigmoid(g) * u).astype(BF16)

    @pl.when(j == 0)
    def _():
        x1_copy.start()
        t_ref[0] = gated()

    @pl.when(j == 1)
    def _():
        x1_copy.wait()

    @pl.when(jnp.logical_and(j > 0, j < nj))
    def _():
        t_new = gated()
        o_ref[...] += jnp.dot(t_ref[1 - slot], wd_ref[...], preferred_element_type=F32)
        t_ref[slot] = t_new

    @pl.when(j == nj)
    def _():
        acc = o_ref[...] + jnp.dot(t_ref[1 - slot], wd_ref[...], preferred_element_type=F32)
        o_ref[...] = _rms(acc, gfin_ref[...])


def _ffn(x1, h2, w_gate, w_up, w_down, g_fin, *, tm, tf):
    t, d = x1.shape
    dff = w_gate.shape[1]
    nj = dff // tf
    row = lambda i, j: (i, 0)
    return pl.pallas_call(
        _ffn_kernel,
        grid=(t // tm, nj + 1),
        in_specs=[
            pl.BlockSpec(memory_space=pl.ANY),
            pl.BlockSpec((tm, d), row),
            pl.BlockSpec((d, tf), lambda i, j: (0, jnp.minimum(j, nj - 1))),
            pl.BlockSpec((d, tf), lambda i, j: (0, jnp.minimum(j, nj - 1))),
            pl.BlockSpec((tf, d), lambda i, j: (jnp.maximum(j - 1, 0), 0)),
            pl.BlockSpec((1, d), lambda i, j: (0, 0)),
        ],
        out_specs=pl.BlockSpec((tm, d), row),
        out_shape=jax.ShapeDtypeStruct((t, d), F32),
        scratch_shapes=[pltpu.VMEM((2, tm, tf), BF16), pltpu.SemaphoreType.DMA(())],
        compiler_params=pltpu.CompilerParams(
            dimension_semantics=("arbitrary", "arbitrary"), vmem_limit_bytes=VMEM_LIMIT_BYTES),
        name="ffn",
    )(x1, h2, w_gate, w_up, w_down, g_fin)


def kernel(x, norm_mix, w_in, w_fourier, sgu_norm, w_spatial, b_spatial, w_out, norm_ffn,
           w_gate, w_up, w_down, norm_final):
    b, s, d = x.shape
    depth = w_in.shape[0]
    gw = sgu_norm.shape[1]
    nheads_g = w_spatial.shape[1]
    assert depth == 1, "only the single-layer block is implemented"
    l = 0
    x2 = x.reshape(b * s, d)
    wcs = _fold_fourier_weights(w_fourier[l])
    b_full = jnp.repeat(jnp.transpose(b_spatial[l]), gw // nheads_g, axis=1)
    p, q, yg, wg_bf = _mix_in(
        x2, norm_mix[l][None, :], w_in[l].astype(BF16), wcs, sgu_norm[l][None, :],
        w_spatial[l].astype(BF16), b_full, w_gate[l], tm=512)
    yf, wu_bf, wo_bf = _seq_dft(p, q, w_up[l], w_out[l], batch=b, n1=SEQ_OUTER, tc=256)
    x1, h2, wd_bf = _out_proj(x2, yf, yg, wo_bf, norm_ffn[l][None, :], w_down[l], tm=512)
    out = _ffn(x1, h2, wg_bf, wu_bf, wd_bf, norm_final[None, :], tm=1024, tf=512)
    return out.reshape(b, s, d)
```

```python
import functools
import math

import jax
import jax.numpy as jnp
from jax import lax
from jax.experimental import pallas as pl
from jax.experimental.pallas import tpu as pltpu

EPS = 1e-6
HEAD = 128
CHUNK = 128
F32 = jnp.float32
BF16 = jnp.bfloat16

VMEM_LIMIT_BYTES = 60000 * 1024
LANES = 128
SEQ_OUTER = 16


def _dft_mats(n):
    idx = jnp.arange(n, dtype=jnp.int32)
    assert n & (n - 1) == 0
    ks = jnp.bitwise_and(idx[:, None] * idx[None, :], n - 1)
    ang = ks.astype(F32) * (2.0 * math.pi / n)
    scale = 1.0 / math.sqrt(n)
    return jnp.cos(ang) * scale, jnp.sin(ang) * scale


def _rms(x, g):
    ms = jnp.mean(x * x, axis=-1, keepdims=True)
    return x * lax.rsqrt(ms + EPS) * g


def _gelu_tanh(x):
    c = math.sqrt(2.0 / math.pi)
    return 0.5 * x * (1.0 + jnp.tanh(c * (x + 0.044715 * (x * x * x))))


def _cast_job(w, nsteps, index_map):
    rows, cols = w.shape
    chunk = rows // nsteps
    assert chunk * nsteps == rows and chunk % 16 == 0
    spec = pl.BlockSpec((chunk, cols), index_map)
    return spec, jax.ShapeDtypeStruct(w.shape, BF16)


def _fold_kernel(cd_ref, sd_ref, w_ref, o_ref):
    for hd in range(w_ref.shape[0]):
        w = w_ref[hd]
        wc = jnp.dot(cd_ref[...], w, preferred_element_type=F32, precision=lax.Precision.HIGHEST)
        ws = jnp.dot(sd_ref[...], w, preferred_element_type=F32, precision=lax.Precision.HIGHEST)
        o_ref[hd, :, :HEAD] = wc.astype(BF16)
        o_ref[hd, :, HEAD:] = ws.astype(BF16)


def _fold_fourier_weights(w_fourier):
    nh = w_fourier.shape[0]
    cd, sd = _dft_mats(HEAD)
    return pl.pallas_call(
        _fold_kernel,
        out_shape=jax.ShapeDtypeStruct((nh, HEAD, 2 * HEAD), BF16),
        name="fold_fourier_weights",
    )(cd, sd, w_fourier)


def _mix_in_kernel(x_ref, gmix_ref, win_ref, wcs_ref, gsgu_ref, wsp_ref, bsp_ref, *rest, fw, gw, ncast):
    cast_refs = rest[:ncast]
    p_ref, q_ref, yg_ref = rest[ncast:ncast + 3]
    cast_out_refs = rest[ncast + 3:2 * ncast + 3]
    z_ref = rest[2 * ncast + 3]
    tm = x_ref.shape[0]
    for src_ref, dst_ref in zip(cast_refs, cast_out_refs):
        dst_ref[...] = src_ref[...].astype(BF16)
    nheads_f = fw // HEAD
    nheads_g = gw // HEAD
    nchunk = tm // CHUNK

    h = _rms(x_ref[...], gmix_ref[...]).astype(BF16)
    z_ref[...] = jnp.dot(h, win_ref[...], preferred_element_type=F32)

    for hd in range(nheads_f):
        cols = slice(hd * HEAD, (hd + 1) * HEAD)
        pq = jnp.dot(z_ref[:, cols].astype(BF16), wcs_ref[hd], preferred_element_type=F32)
        p_ref[:, cols] = pq[:, :HEAD].astype(BF16)
        q_ref[:, cols] = pq[:, HEAD:].astype(BF16)

    v = _rms(_gelu_tanh(z_ref[:, fw + gw:]), gsgu_ref[...]).astype(BF16)
    for hd in range(nheads_g):
        cols = slice(hd * HEAD, (hd + 1) * HEAD)
        vcat = jnp.concatenate(
            [v[n * CHUNK:(n + 1) * CHUNK, cols] for n in range(nchunk)], axis=1)
        sv = jnp.dot(wsp_ref[hd], vcat, preferred_element_type=F32)
        bias = bsp_ref[:, cols]
        for n in range(nchunk):
            rows = slice(n * CHUNK, (n + 1) * CHUNK)
            u = _gelu_tanh(z_ref[rows, fw + hd * HEAD: fw + (hd + 1) * HEAD])
            yg_ref[rows, cols] = (u * (sv[:, n * HEAD:(n + 1) * HEAD] + bias)).astype(BF16)


def _mix_in(x2, g_mix, w_in, wcs, g_sgu, w_sp, b_full, w_casts, *, tm):
    t, d = x2.shape
    cast_jobs = [_cast_job(w, t // tm, lambda i: (i, 0)) for w in w_casts]
    cast_specs = [job[0] for job in cast_jobs]
    cast_shapes = [job[1] for job in cast_jobs]
    nin = w_in.shape[1]
    gw = g_sgu.shape[1]
    fw = nin - 2 * gw
    const2 = lambda i: (0, 0)
    const3 = lambda i: (0, 0, 0)
    single = pl.Buffered(1)
    return pl.pallas_call(
        functools.partial(_mix_in_kernel, fw=fw, gw=gw, ncast=len(w_casts)),
        grid=(t // tm,),
        in_specs=[
            pl.BlockSpec((tm, d), lambda i: (i, 0)),
            pl.BlockSpec((1, d), const2),
            pl.BlockSpec((d, nin), const2, pipeline_mode=single),
            pl.BlockSpec(wcs.shape, const3),
            pl.BlockSpec((1, gw), const2),
            pl.BlockSpec(w_sp.shape, const3),
            pl.BlockSpec(b_full.shape, const2),
            *cast_specs,
        ],
        out_specs=[
            pl.BlockSpec((tm, fw), lambda i: (i, 0)),
            pl.BlockSpec((tm, fw), lambda i: (i, 0)),
            pl.BlockSpec((tm, gw), lambda i: (i, 0)),
            *cast_specs,
        ],
        out_shape=[
            jax.ShapeDtypeStruct((t, fw), BF16),
            jax.ShapeDtypeStruct((t, fw), BF16),
            jax.ShapeDtypeStruct((t, gw), BF16),
            *cast_shapes,
        ],
        scratch_shapes=[pltpu.VMEM((tm, nin), F32)],
        compiler_params=pltpu.CompilerParams(
            dimension_semantics=("arbitrary",), vmem_limit_bytes=VMEM_LIMIT_BYTES),
        name="mix_in",
    )(x2, g_mix, w_in, wcs, g_sgu, w_sp, b_full, *w_casts)


def _stage2_matrices(n1, n2):
    n = n1 * n2
    assert n & (n - 1) == 0
    k1 = jnp.arange(n1, dtype=jnp.int32)[:, None, None]
    k2 = jnp.arange(n2, dtype=jnp.int32)[None, :, None]
    s2 = jnp.arange(n2, dtype=jnp.int32)[None, None, :]
    alpha = jnp.bitwise_and(s2 * k1, n - 1).astype(F32) * (2.0 * math.pi / n)
    beta = jnp.bitwise_and(s2 * k2, n2 - 1).astype(F32) * (2.0 * math.pi / n2)
    ca, sa, cb, sb = jnp.cos(alpha), jnp.sin(alpha), jnp.cos(beta), jnp.sin(beta)
    g = jnp.concatenate([ca * cb - sa * sb, -(sa * cb + ca * sb)], axis=-1) * (1.0 / math.sqrt(n))
    return g.astype(BF16)


def _fft_pos(xr, xi):
    n = len(xr)
    if n == 1:
        return xr, xi
    er, ei = _fft_pos(xr[0::2], xi[0::2])
    odr, odi = _fft_pos(xr[1::2], xi[1::2])
    outr, outi = [None] * n, [None] * n
    half = n // 2
    for k in range(half):
        a, b = odr[k], odi[k]
        if k == 0:
            tr, ti = a, b
        elif 4 * k == n:
            tr, ti = -b, a
        elif 8 * k == n:
            tr, ti = (a - b) * math.sqrt(0.5), (a + b) * math.sqrt(0.5)
        elif 8 * k == 3 * n:
            tr, ti = (a + b) * -math.sqrt(0.5), (a - b) * math.sqrt(0.5)
        else:
            c, s = math.cos(2.0 * math.pi * k / n), math.sin(2.0 * math.pi * k / n)
            tr, ti = a * c - b * s, b * c + a * s
        outr[k], outi[k] = er[k] + tr, ei[k] + ti
        outr[k + half], outi[k + half] = er[k] - tr, ei[k] - ti
    return outr, outi


def _seq_dft_kernel(g_ref, p_ref, q_ref, cast_ref, o_ref, cast_out_ref, t_ref, r_ref, *, n1, n2, pitch):
    cast_out_ref[...] = cast_ref[...].astype(BF16)
    tc = p_ref.shape[-1]
    nlt = tc // LANES
    rb = 16

    def stage1(i, carry):
        r0 = pl.multiple_of(i * rb, rb)
        for j in range(nlt):
            lanes = slice(j * LANES, (j + 1) * LANES)
            zr = [p_ref[0, pl.ds(s1 * n2 + r0, rb), lanes].astype(F32) for s1 in range(n1)]
            zi = [q_ref[0, pl.ds(s1 * n2 + r0, rb), lanes].astype(F32) for s1 in range(n1)]
            tr, ti = _fft_pos(zr, zi)
            for k1 in range(n1):
                t_ref[k1, pl.ds(r0, rb), lanes] = tr[k1].astype(BF16)
                t_ref[k1, pl.ds(n2 + r0, rb), lanes] = ti[k1].astype(BF16)
        return carry

    lax.fori_loop(0, n2 // rb, stage1, 0)

    for k1 in range(n1):
        res = jnp.dot(g_ref[k1], t_ref[k1], preferred_element_type=F32)
        for j in range(nlt):
            r_ref[j, k1 * pitch:k1 * pitch + n2, :] = res[:, j * LANES:(j + 1) * LANES]

    def interleave(k2, carry):
        row = pl.multiple_of(k2 * n1, n1)
        for j in range(nlt):
            blk = r_ref[j, pl.ds(k2, n1, stride=pitch), :]
            o_ref[0, pl.ds(row, n1), j * LANES:(j + 1) * LANES] = blk.astype(o_ref.dtype)
        return carry

    lax.fori_loop(0, n2, interleave, 0, unroll=8)


def _seq_dft(p, q, w_cast, *, batch, n1, tc):
    t, w = p.shape
    nj = w // tc
    step = lambda b, j: (b * nj + j, 0)
    cast_spec, cast_shape = _cast_job(w_cast, batch * nj, step)
    s = t // batch
    n2 = s // n1
    pitch = n2 + 8
    g2 = _stage2_matrices(n1, n2)
    blk = pl.BlockSpec((1, s, tc), lambda b, j: (b, 0, j))
    yf, w_bf = pl.pallas_call(
        functools.partial(_seq_dft_kernel, n1=n1, n2=n2, pitch=pitch),
        grid=(batch, nj),
        in_specs=[pl.BlockSpec(g2.shape, lambda b, j: (0, 0, 0), pipeline_mode=pl.Buffered(1)),
                  blk, blk, cast_spec],
        out_specs=[blk, cast_spec],
        out_shape=[jax.ShapeDtypeStruct((batch, s, w), BF16), cast_shape],
        scratch_shapes=[pltpu.VMEM((n1, 2 * n2, tc), BF16),
                        pltpu.VMEM((tc // LANES, n1 * pitch, LANES), F32)],
        compiler_params=pltpu.CompilerParams(
            dimension_semantics=("arbitrary", "arbitrary"), vmem_limit_bytes=VMEM_LIMIT_BYTES),
        name="seq_dft",
    )(g2, p.reshape(batch, s, w), q.reshape(batch, s, w), w_cast)
    return yf.reshape(t, w), w_bf


def _out_proj_kernel(x_ref, yf_ref, yg_ref, wo_ref, gffn_ref, x1_ref, h2_ref):
    fw = yf_ref.shape[1]
    x1 = x_ref[...]
    x1 = x1 + jnp.dot(yf_ref[...], wo_ref[:fw, :], preferred_element_type=F32)
    x1 = x1 + jnp.dot(yg_ref[...], wo_ref[fw:, :], preferred_element_type=F32)
    x1_ref[...] = x1
    h2_ref[...] = _rms(x1, gffn_ref[...]).astype(BF16)


def _out_proj(x2, yf, yg, w_out, g_ffn, *, tm):
    t, d = x2.shape
    fw = yf.shape[1]
    gw = yg.shape[1]
    row = lambda i: (i, 0)
    const = lambda i: (0, 0)
    return pl.pallas_call(
        _out_proj_kernel,
        grid=(t // tm,),
        in_specs=[
            pl.BlockSpec((tm, d), row),
            pl.BlockSpec((tm, fw), row),
            pl.BlockSpec((tm, gw), row),
            pl.BlockSpec((fw + gw, d), const, pipeline_mode=pl.Buffered(1)),
            pl.BlockSpec((1, d), const),
        ],
        out_specs=[pl.BlockSpec((tm, d), row), pl.BlockSpec((tm, d), row)],
        out_shape=[jax.ShapeDtypeStruct((t, d), F32), jax.ShapeDtypeStruct((t, d), BF16)],
        compiler_params=pltpu.CompilerParams(
            dimension_semantics=("arbitrary",), vmem_limit_bytes=VMEM_LIMIT_BYTES),
        name="out_proj",
    )(x2, yf, yg, w_out, g_ffn)


def _ffn_kernel(x1_hbm, h2_ref, wg_ref, wu_ref, wd_ref, gfin_ref, o_ref, t_ref, sem):
    i = pl.program_id(0)
    j = pl.program_id(1)
    nj = pl.num_programs(1) - 1
    tm = o_ref.shape[0]
    slot = j % 2
    x1_copy = pltpu.make_async_copy(x1_hbm.at[pl.ds(i * tm, tm), :], o_ref, sem)

    def gated():
        g = jnp.dot(h2_ref[...], wg_ref[...], preferred_element_type=F32)
        u = jnp.dot(h2_ref[...], wu_ref[...], preferred_element_type=F32)
        return (g * jax.nn.sigmoid(g) * u).astype(BF16)

    @pl.when(j == 0)
    def _():
        x1_copy.start()
        t_ref[0] = gated()

    @pl.when(j == 1)
    def _():
        x1_copy.wait()

    @pl.when(jnp.logical_and(j > 0, j < nj))
    def _():
        t_new = gated()
        o_ref[...] += jnp.dot(t_ref[1 - slot], wd_ref[...], preferred_element_type=F32)
        t_ref[slot] = t_new

    @pl.when(j == nj)
    def _():
        acc = o_ref[...] + jnp.dot(t_ref[1 - slot], wd_ref[...], preferred_element_type=F32)
        o_ref[...] = _rms(acc, gfin_ref[...])


def _ffn(x1, h2, w_gate, w_up, w_down, g_fin, *, tm, tf):
    t, d = x1.shape
    dff = w_gate.shape[1]
    nj = dff // tf
    row = lambda i, j: (i, 0)
    return pl.pallas_call(
        _ffn_kernel,
        grid=(t // tm, nj + 1),
        in_specs=[
            pl.BlockSpec(memory_space=pl.ANY),
            pl.BlockSpec((tm, d), row),
            pl.BlockSpec((d, tf), lambda i, j: (0, jnp.minimum(j, nj - 1))),
            pl.BlockSpec((d, tf), lambda i, j: (0, jnp.minimum(j, nj - 1))),
            pl.BlockSpec((tf, d), lambda i, j: (jnp.maximum(j - 1, 0), 0)),
            pl.BlockSpec((1, d), lambda i, j: (0, 0)),
        ],
        out_specs=pl.BlockSpec((tm, d), row),
        out_shape=jax.ShapeDtypeStruct((t, d), F32),
        scratch_shapes=[pltpu.VMEM((2, tm, tf), BF16), pltpu.SemaphoreType.DMA(())],
        compiler_params=pltpu.CompilerParams(
            dimension_semantics=("arbitrary", "arbitrary"), vmem_limit_bytes=VMEM_LIMIT_BYTES),
        name="ffn",
    )(x1, h2, w_gate, w_up, w_down, g_fin)


def kernel(x, norm_mix, w_in, w_fourier, sgu_norm, w_spatial, b_spatial, w_out, norm_ffn,
           w_gate, w_up, w_down, norm_final):
    b, s, d = x.shape
    depth = w_in.shape[0]
    gw = sgu_norm.shape[1]
    nheads_g = w_spatial.shape[1]
    assert depth == 1, "only the single-layer block is implemented"
    l = 0
    x2 = x.reshape(b * s, d)
    wcs = _fold_fourier_weights(w_fourier[l])
    b_full = jnp.repeat(jnp.transpose(b_spatial[l]), gw // nheads_g, axis=1)
    p, q, yg, wg_bf, wd_bf, wo_bf = _mix_in(
        x2, norm_mix[l][None, :], w_in[l].astype(BF16), wcs, sgu_norm[l][None, :],
        w_spatial[l].astype(BF16), b_full, (w_gate[l], w_down[l], w_out[l]), tm=512)
    yf, wu_bf = _seq_dft(p, q, w_up[l], batch=b, n1=SEQ_OUTER, tc=256)
    x1, h2 = _out_proj(x2, yf, yg, wo_bf, norm_ffn[l][None, :], tm=512)
    out = _ffn(x1, h2, wg_bf, wu_bf, wd_bf, norm_final[None, :], tm=1024, tf=512)
    return out.reshape(b, s, d)
```

```python
import functools
import math

import jax
import jax.numpy as jnp
from jax import lax
from jax.experimental import pallas as pl
from jax.experimental.pallas import tpu as pltpu

EPS = 1e-6
HEAD = 128
CHUNK = 128
F32 = jnp.float32
BF16 = jnp.bfloat16

VMEM_LIMIT_BYTES = 60000 * 1024
LANES = 128
MXU_WIDTH = 256
SEQ_OUTER = 16


def _dft_mats(n):
    idx = jnp.arange(n, dtype=jnp.int32)
    assert n & (n - 1) == 0
    ks = jnp.bitwise_and(idx[:, None] * idx[None, :], n - 1)
    ang = ks.astype(F32) * (2.0 * math.pi / n)
    scale = 1.0 / math.sqrt(n)
    return jnp.cos(ang) * scale, jnp.sin(ang) * scale


def _rms(x, g):
    ms = jnp.mean(x * x, axis=-1, keepdims=True)
    return x * lax.rsqrt(ms + EPS) * g


def _gelu_tanh(x):
    c = math.sqrt(2.0 / math.pi)
    return 0.5 * x * (1.0 + jnp.tanh(c * (x + 0.044715 * (x * x * x))))


def _cast_job(w, nsteps, index_map):
    rows, cols = w.shape
    chunk = rows // nsteps
    assert chunk * nsteps == rows and chunk % 16 == 0
    spec = pl.BlockSpec((chunk, cols), index_map)
    return spec, jax.ShapeDtypeStruct(w.shape, BF16)


def _fold_kernel(cd_ref, sd_ref, w_ref, win_ref, o_ref, win_out_ref, *, fw, gw):
    @pl.when(pl.program_id(0) == 0)
    def _():
        for hd in range(w_ref.shape[0]):
            w = w_ref[hd]
            wc = jnp.dot(cd_ref[...], w, preferred_element_type=F32, precision=lax.Precision.HIGHEST)
            ws = jnp.dot(sd_ref[...], w, preferred_element_type=F32, precision=lax.Precision.HIGHEST)
            o_ref[hd, :, :HEAD] = wc.astype(BF16)
            o_ref[hd, :, HEAD:] = ws.astype(BF16)

    win_out_ref[:, :gw] = win_ref[:, fw + gw:].astype(BF16)
    win_out_ref[:, gw:2 * gw] = win_ref[:, fw:fw + gw].astype(BF16)
    win_out_ref[:, 2 * gw:] = win_ref[:, :fw].astype(BF16)


def _fold_fourier_weights(w_fourier, w_in, *, fw, gw, nsteps):
    nh = w_fourier.shape[0]
    cd, sd = _dft_mats(HEAD)
    win_spec, win_shape = _cast_job(w_in, nsteps, lambda i: (i, 0))
    const2 = lambda i: (0, 0)
    const3 = lambda i: (0, 0, 0)
    return pl.pallas_call(
        functools.partial(_fold_kernel, fw=fw, gw=gw),
        grid=(nsteps,),
        in_specs=[pl.BlockSpec(cd.shape, const2), pl.BlockSpec(sd.shape, const2),
                  pl.BlockSpec(w_fourier.shape, const3), win_spec],
        out_specs=[pl.BlockSpec((nh, HEAD, 2 * HEAD), const3), win_spec],
        out_shape=[jax.ShapeDtypeStruct((nh, HEAD, 2 * HEAD), BF16), win_shape],
        compiler_params=pltpu.CompilerParams(dimension_semantics=("arbitrary",)),
        name="fold_fourier_weights",
    )(cd, sd, w_fourier, w_in)


def _mix_in_kernel(x_ref, gmix_ref, win_ref, wcs_ref, gsgu_ref, wsp_ref, bsp_ref, *rest, fw, gw, ncast):
    cast_refs = rest[:ncast]
    p_ref, q_ref, yg_ref = rest[ncast:ncast + 3]
    cast_out_refs = rest[ncast + 3:2 * ncast + 3]
    z_ref = rest[2 * ncast + 3]
    tm = x_ref.shape[0]
    for src_ref, dst_ref in zip(cast_refs, cast_out_refs):
        dst_ref[...] = src_ref[...].astype(BF16)
    nheads_f = fw // HEAD
    nheads_g = gw // HEAD
    nchunk = tm // CHUNK

    h = _rms(x_ref[...], gmix_ref[...]).astype(BF16)
    z_ref[...] = jnp.dot(h, win_ref[...], preferred_element_type=F32)

    v = _rms(_gelu_tanh(z_ref[:, :gw]), gsgu_ref[...]).astype(BF16)
    for hd in range(nheads_g):
        cols = slice(hd * HEAD, (hd + 1) * HEAD)
        vcat = jnp.concatenate(
            [v[n * CHUNK:(n + 1) * CHUNK, cols] for n in range(nchunk)], axis=1)
        sv = jnp.dot(wsp_ref[hd], vcat, preferred_element_type=F32)
        bias = bsp_ref[:, cols]
        for n in range(nchunk):
            rows = slice(n * CHUNK, (n + 1) * CHUNK)
            u = _gelu_tanh(z_ref[rows, gw + hd * HEAD: gw + (hd + 1) * HEAD])
            yg_ref[rows, cols] = (u * (sv[:, n * HEAD:(n + 1) * HEAD] + bias)).astype(BF16)

    for hd in range(nheads_f):
        cols = slice(hd * HEAD, (hd + 1) * HEAD)
        a = z_ref[:, 2 * gw + hd * HEAD: 2 * gw + (hd + 1) * HEAD].astype(BF16)
        pq = jnp.dot(a, wcs_ref[hd], preferred_element_type=F32)
        p_ref[:, cols] = pq[:, :HEAD].astype(BF16)
        q_ref[:, cols] = pq[:, HEAD:].astype(BF16)


def _mix_in(x2, g_mix, w_in, wcs, g_sgu, w_sp, b_full, w_casts, *, tm):
    t, d = x2.shape
    cast_jobs = [_cast_job(w, t // tm, lambda i: (i, 0)) for w in w_casts]
    cast_specs = [job[0] for job in cast_jobs]
    cast_shapes = [job[1] for job in cast_jobs]
    nin = w_in.shape[1]
    gw = g_sgu.shape[1]
    fw = nin - 2 * gw
    const2 = lambda i: (0, 0)
    const3 = lambda i: (0, 0, 0)
    single = pl.Buffered(1)
    return pl.pallas_call(
        functools.partial(_mix_in_kernel, fw=fw, gw=gw, ncast=len(w_casts)),
        grid=(t // tm,),
        in_specs=[
            pl.BlockSpec((tm, d), lambda i: (i, 0)),
            pl.BlockSpec((1, d), const2),
            pl.BlockSpec((d, nin), const2, pipeline_mode=single),
            pl.BlockSpec(wcs.shape, const3),
            pl.BlockSpec((1, gw), const2),
            pl.BlockSpec(w_sp.shape, const3),
            pl.BlockSpec(b_full.shape, const2),
            *cast_specs,
        ],
        out_specs=[
            pl.BlockSpec((tm, fw), lambda i: (i, 0)),
            pl.BlockSpec((tm, fw), lambda i: (i, 0)),
            pl.BlockSpec((tm, gw), lambda i: (i, 0)),
            *cast_specs,
        ],
        out_shape=[
            jax.ShapeDtypeStruct((t, fw), BF16),
            jax.ShapeDtypeStruct((t, fw), BF16),
            jax.ShapeDtypeStruct((t, gw), BF16),
            *cast_shapes,
        ],
        scratch_shapes=[pltpu.VMEM((tm, nin), F32)],
        compiler_params=pltpu.CompilerParams(
            dimension_semantics=("arbitrary",), vmem_limit_bytes=VMEM_LIMIT_BYTES),
        name="mix_in",
    )(x2, g_mix, w_in, wcs, g_sgu, w_sp, b_full, *w_casts)


def _stage2_matrices(n1, n2):
    n = n1 * n2
    assert n & (n - 1) == 0
    k1 = jnp.arange(n1, dtype=jnp.int32)[:, None, None]
    k2 = jnp.arange(n2, dtype=jnp.int32)[None, :, None]
    s2 = jnp.arange(n2, dtype=jnp.int32)[None, None, :]
    alpha = jnp.bitwise_and(s2 * k1, n - 1).astype(F32) * (2.0 * math.pi / n)
    beta = jnp.bitwise_and(s2 * k2, n2 - 1).astype(F32) * (2.0 * math.pi / n2)
    ca, sa, cb, sb = jnp.cos(alpha), jnp.sin(alpha), jnp.cos(beta), jnp.sin(beta)
    g = jnp.concatenate([ca * cb - sa * sb, -(sa * cb + ca * sb)], axis=-1) * (1.0 / math.sqrt(n))
    return g.astype(BF16)


def _fft_pos(xr, xi):
    n = len(xr)
    if n == 1:
        return xr, xi
    er, ei = _fft_pos(xr[0::2], xi[0::2])
    odr, odi = _fft_pos(xr[1::2], xi[1::2])
    outr, outi = [None] * n, [None] * n
    half = n // 2
    for k in range(half):
        a, b = odr[k], odi[k]
        if k == 0:
            tr, ti = a, b
        elif 4 * k == n:
            tr, ti = -b, a
        elif 8 * k == n:
            tr, ti = (a - b) * math.sqrt(0.5), (a + b) * math.sqrt(0.5)
        elif 8 * k == 3 * n:
            tr, ti = (a + b) * -math.sqrt(0.5), (a - b) * math.sqrt(0.5)
        else:
            c, s = math.cos(2.0 * math.pi * k / n), math.sin(2.0 * math.pi * k / n)
            tr, ti = a * c - b * s, b * c + a * s
        outr[k], outi[k] = er[k] + tr, ei[k] + ti
        outr[k + half], outi[k + half] = er[k] - tr, ei[k] - ti
    return outr, outi


def _seq_dft_kernel(g_ref, p_ref, q_ref, cast_ref, o_ref, cast_out_ref, t_ref, r_ref, *, n1, n2, pitch):
    cast_out_ref[...] = cast_ref[...].astype(BF16)
    tc = p_ref.shape[-1]
    nlt = tc // LANES
    rb = 16

    def stage1(i, carry):
        r0 = pl.multiple_of(i * rb, rb)
        for j in range(nlt):
            lanes = slice(j * LANES, (j + 1) * LANES)
            zr = [p_ref[0, pl.ds(s1 * n2 + r0, rb), lanes].astype(F32) for s1 in range(n1)]
            zi = [q_ref[0, pl.ds(s1 * n2 + r0, rb), lanes].astype(F32) for s1 in range(n1)]
            tr, ti = _fft_pos(zr, zi)
            for k1 in range(n1):
                t_ref[k1, pl.ds(r0, rb), lanes] = tr[k1].astype(BF16)
                t_ref[k1, pl.ds(n2 + r0, rb), lanes] = ti[k1].astype(BF16)
        return carry

    lax.fori_loop(0, n2 // rb, stage1, 0)

    for k1 in range(n1):
        res = jnp.dot(g_ref[k1], t_ref[k1], preferred_element_type=F32)
        for j in range(nlt):
            r_ref[j, k1 * pitch:k1 * pitch + n2, :] = res[:, j * LANES:(j + 1) * LANES]

    def interleave(k2, carry):
        row = pl.multiple_of(k2 * n1, n1)
        for j in range(nlt):
            blk = r_ref[j, pl.ds(k2, n1, stride=pitch), :]
            o_ref[0, pl.ds(row, n1), j * LANES:(j + 1) * LANES] = blk.astype(o_ref.dtype)
        return carry

    lax.fori_loop(0, n2, interleave, 0, unroll=8)


def _seq_dft(p, q, w_cast, *, batch, n1, tc):
    t, w = p.shape
    nj = w // tc
    step = lambda b, j: (b * nj + j, 0)
    cast_spec, cast_shape = _cast_job(w_cast, batch * nj, step)
    s = t // batch
    n2 = s // n1
    pitch = n2 + 8
    g2 = _stage2_matrices(n1, n2)
    blk = pl.BlockSpec((1, s, tc), lambda b, j: (b, 0, j))
    yf, w_bf = pl.pallas_call(
        functools.partial(_seq_dft_kernel, n1=n1, n2=n2, pitch=pitch),
        grid=(batch, nj),
        in_specs=[pl.BlockSpec(g2.shape, lambda b, j: (0, 0, 0), pipeline_mode=pl.Buffered(1)),
                  blk, blk, cast_spec],
        out_specs=[blk, cast_spec],
        out_shape=[jax.ShapeDtypeStruct((batch, s, w), BF16), cast_shape],
        scratch_shapes=[pltpu.VMEM((n1, 2 * n2, tc), BF16),
                        pltpu.VMEM((tc // LANES, n1 * pitch, LANES), F32)],
        compiler_params=pltpu.CompilerParams(
            dimension_semantics=("arbitrary", "arbitrary"), vmem_limit_bytes=VMEM_LIMIT_BYTES),
        name="seq_dft",
    )(g2, p.reshape(batch, s, w), q.reshape(batch, s, w), w_cast)
    return yf.reshape(t, w), w_bf


def _out_proj_kernel(x_ref, yf_ref, yg_ref, wo_ref, gffn_ref, x1_ref, h2_ref):
    fw = yf_ref.shape[1]
    for c in range(x1_ref.shape[1] // MXU_WIDTH):
        cols = slice(c * MXU_WIDTH, (c + 1) * MXU_WIDTH)
        x1_ref[:, cols] = (x_ref[:, cols]
                           + jnp.dot(yf_ref[...], wo_ref[:fw, cols], preferred_element_type=F32)
                           + jnp.dot(yg_ref[...], wo_ref[fw:, cols], preferred_element_type=F32))
    h2_ref[...] = _rms(x1_ref[...], gffn_ref[...]).astype(BF16)


def _out_proj(x2, yf, yg, w_out, g_ffn, *, tm):
    t, d = x2.shape
    fw = yf.shape[1]
    gw = yg.shape[1]
    row = lambda i: (i, 0)
    const = lambda i: (0, 0)
    return pl.pallas_call(
        _out_proj_kernel,
        grid=(t // tm,),
        in_specs=[
            pl.BlockSpec((tm, d), row),
            pl.BlockSpec((tm, fw), row),
            pl.BlockSpec((tm, gw), row),
            pl.BlockSpec((fw + gw, d), const, pipeline_mode=pl.Buffered(1)),
            pl.BlockSpec((1, d), const),
        ],
        out_specs=[pl.BlockSpec((tm, d), row), pl.BlockSpec((tm, d), row)],
        out_shape=[jax.ShapeDtypeStruct((t, d), F32), jax.ShapeDtypeStruct((t, d), BF16)],
        compiler_params=pltpu.CompilerParams(
            dimension_semantics=("arbitrary",), vmem_limit_bytes=VMEM_LIMIT_BYTES),
        name="out_proj",
    )(x2, yf, yg, w_out, g_ffn)


def _ffn_kernel(x1_hbm, h2_ref, wg_ref, wu_ref, wd_ref, gfin_ref, o_ref, t_ref, sem):
    i = pl.program_id(0)
    j = pl.program_id(1)
    nj = pl.num_programs(1) - 1
    tm = o_ref.shape[0]
    slot = j % 2
    x1_copy = pltpu.make_async_copy(x1_hbm.at[pl.ds(i * tm, tm), :], o_ref, sem)

    def gated(dst_slot):
        for c in range(wg_ref.shape[1] // MXU_WIDTH):
            cols = slice(c * MXU_WIDTH, (c + 1) * MXU_WIDTH)
            g = jnp.dot(h2_ref[...], wg_ref[:, cols], preferred_element_type=F32)
            u = jnp.dot(h2_ref[...], wu_ref[:, cols], preferred_element_type=F32)
            t_ref[dst_slot, :, cols] = (g * jax.nn.sigmoid(g) * u).astype(BF16)

    def down(src_slot):
        for c in range(o_ref.shape[1] // MXU_WIDTH):
            cols = slice(c * MXU_WIDTH, (c + 1) * MXU_WIDTH)
            o_ref[:, cols] += jnp.dot(t_ref[src_slot], wd_ref[:, cols], preferred_element_type=F32)

    @pl.when(j == 0)
    def _():
        x1_copy.start()
        gated(0)

    @pl.when(j == 1)
    def _():
        x1_copy.wait()

    @pl.when(jnp.logical_and(j > 0, j < nj))
    def _():
        down(1 - slot)
        gated(slot)

    @pl.when(j == nj)
    def _():
        down(1 - slot)
        o_ref[...] = _rms(o_ref[...], gfin_ref[...])


def _ffn(x1, h2, w_gate, w_up, w_down, g_fin, *, tm, tf):
    t, d = x1.shape
    dff = w_gate.shape[1]
    nj = dff // tf
    row = lambda i, j: (i, 0)
    return pl.pallas_call(
        _ffn_kernel,
        grid=(t // tm, nj + 1),
        in_specs=[
            pl.BlockSpec(memory_space=pl.ANY),
            pl.BlockSpec((tm, d), row),
            pl.BlockSpec((d, tf), lambda i, j: (0, jnp.minimum(j, nj - 1))),
            pl.BlockSpec((d, tf), lambda i, j: (0, jnp.minimum(j, nj - 1))),
            pl.BlockSpec((tf, d), lambda i, j: (jnp.maximum(j - 1, 0), 0)),
            pl.BlockSpec((1, d), lambda i, j: (0, 0)),
        ],
        out_specs=pl.BlockSpec((tm, d), row),
        out_shape=jax.ShapeDtypeStruct((t, d), F32),
        scratch_shapes=[pltpu.VMEM((2, tm, tf), BF16), pltpu.SemaphoreType.DMA(())],
        compiler_params=pltpu.CompilerParams(
            dimension_semantics=("arbitrary", "arbitrary"), vmem_limit_bytes=VMEM_LIMIT_BYTES),
        name="ffn",
    )(x1, h2, w_gate, w_up, w_down, g_fin)


def kernel(x, norm_mix, w_in, w_fourier, sgu_norm, w_spatial, b_spatial, w_out, norm_ffn,
           w_gate, w_up, w_down, norm_final):
    b, s, d = x.shape
    depth = w_in.shape[0]
    gw = sgu_norm.shape[1]
    nheads_g = w_spatial.shape[1]
    assert depth == 1, "only the single-layer block is implemented"
    l = 0
    x2 = x.reshape(b * s, d)
    fw = w_in.shape[2] - 2 * gw
    wcs, win_bf = _fold_fourier_weights(w_fourier[l], w_in[l], fw=fw, gw=gw, nsteps=8)
    b_full = jnp.repeat(jnp.transpose(b_spatial[l]), gw // nheads_g, axis=1)
    p, q, yg, wg_bf, wd_bf, wo_bf = _mix_in(
        x2, norm_mix[l][None, :], win_bf, wcs, sgu_norm[l][None, :],
        w_spatial[l].astype(BF16), b_full, (w_gate[l], w_down[l], w_out[l]), tm=512)
    yf, wu_bf = _seq_dft(p, q, w_up[l], batch=b, n1=SEQ_OUTER, tc=256)
    x1, h2 = _out_proj(x2, yf, yg, wo_bf, norm_ffn[l][None, :], tm=512)
    out = _ffn(x1, h2, wg_bf, wu_bf, wd_bf, norm_final[None, :], tm=1024, tf=512)
    return out.reshape(b, s, d)
```

```python
import functools
import math

import jax
import jax.numpy as jnp
from jax import lax
from jax.experimental import pallas as pl
from jax.experimental.pallas import tpu as pltpu

EPS = 1e-6
HEAD = 128
CHUNK = 128
F32 = jnp.float32
BF16 = jnp.bfloat16

VMEM_LIMIT_BYTES = 60000 * 1024
LANES = 128
MXU_WIDTH = 256
SEQ_OUTER = 16


def _dft_mats(n):
    idx = jnp.arange(n, dtype=jnp.int32)
    assert n & (n - 1) == 0
    ks = jnp.bitwise_and(idx[:, None] * idx[None, :], n - 1)
    ang = ks.astype(F32) * (2.0 * math.pi / n)
    scale = 1.0 / math.sqrt(n)
    return jnp.cos(ang) * scale, jnp.sin(ang) * scale


def _rms(x, g):
    ms = jnp.mean(x * x, axis=-1, keepdims=True)
    return x * lax.rsqrt(ms + EPS) * g


def _gelu_tanh(x):
    c = math.sqrt(2.0 / math.pi)
    return 0.5 * x * (1.0 + jnp.tanh(c * (x + 0.044715 * (x * x * x))))


def _fold_kernel(cd_ref, sd_ref, w_ref, o_ref):
    for hd in range(w_ref.shape[0]):
        w = w_ref[hd]
        wc = jnp.dot(cd_ref[...], w, preferred_element_type=F32, precision=lax.Precision.HIGHEST)
        ws = jnp.dot(sd_ref[...], w, preferred_element_type=F32, precision=lax.Precision.HIGHEST)
        o_ref[hd, :, :HEAD] = wc.astype(BF16)
        o_ref[hd, :, HEAD:] = ws.astype(BF16)


def _fold_fourier_weights(w_fourier):
    nh = w_fourier.shape[0]
    cd, sd = _dft_mats(HEAD)
    return pl.pallas_call(
        _fold_kernel,
        out_shape=jax.ShapeDtypeStruct((nh, HEAD, 2 * HEAD), BF16),
        name="fold_fourier_weights",
    )(cd, sd, w_fourier)


def _mix_in_kernel(x_ref, gmix_ref, win_ref, wcs_ref, gsgu_ref, wsp_ref, bsp_ref,
                   p_ref, q_ref, yg_ref, z_ref, *, fw, gw):
    tm = x_ref.shape[0]
    nheads_f = fw // HEAD
    nheads_g = gw // HEAD
    nchunk = tm // CHUNK

    h = _rms(x_ref[...], gmix_ref[...]).astype(BF16)
    z_ref[:, :gw] = jnp.dot(h, win_ref[:, fw + gw:].astype(BF16), preferred_element_type=F32)
    z_ref[:, gw:2 * gw] = jnp.dot(h, win_ref[:, fw:fw + gw].astype(BF16), preferred_element_type=F32)
    z_ref[:, 2 * gw:] = jnp.dot(h, win_ref[:, :fw].astype(BF16), preferred_element_type=F32)

    v = _rms(_gelu_tanh(z_ref[:, :gw]), gsgu_ref[...]).astype(BF16)
    for hd in range(nheads_g):
        cols = slice(hd * HEAD, (hd + 1) * HEAD)
        vcat = jnp.concatenate(
            [v[n * CHUNK:(n + 1) * CHUNK, cols] for n in range(nchunk)], axis=1)
        sv = jnp.dot(wsp_ref[hd], vcat, preferred_element_type=F32)
        bias = bsp_ref[:, cols]
        for n in range(nchunk):
            rows = slice(n * CHUNK, (n + 1) * CHUNK)
            u = _gelu_tanh(z_ref[rows, gw + hd * HEAD: gw + (hd + 1) * HEAD])
            yg_ref[rows, cols] = (u * (sv[:, n * HEAD:(n + 1) * HEAD] + bias)).astype(BF16)

    for hd in range(nheads_f):
        cols = slice(hd * HEAD, (hd + 1) * HEAD)
        a = z_ref[:, 2 * gw + hd * HEAD: 2 * gw + (hd + 1) * HEAD].astype(BF16)
        pq = jnp.dot(a, wcs_ref[hd], preferred_element_type=F32)
        p_ref[:, cols] = pq[:, :HEAD].astype(BF16)
        q_ref[:, cols] = pq[:, HEAD:].astype(BF16)


def _mix_in(x2, g_mix, w_in, wcs, g_sgu, w_sp, b_full, *, tm):
    t, d = x2.shape
    nin = w_in.shape[1]
    gw = g_sgu.shape[1]
    fw = nin - 2 * gw
    const2 = lambda i: (0, 0)
    const3 = lambda i: (0, 0, 0)
    single = pl.Buffered(1)
    return pl.pallas_call(
        functools.partial(_mix_in_kernel, fw=fw, gw=gw),
        grid=(t // tm,),
        in_specs=[
            pl.BlockSpec((tm, d), lambda i: (i, 0)),
            pl.BlockSpec((1, d), const2),
            pl.BlockSpec((d, nin), const2, pipeline_mode=single),
            pl.BlockSpec(wcs.shape, const3),
            pl.BlockSpec((1, gw), const2),
            pl.BlockSpec(w_sp.shape, const3),
            pl.BlockSpec(b_full.shape, const2),
        ],
        out_specs=[
            pl.BlockSpec((tm, fw), lambda i: (i, 0)),
            pl.BlockSpec((tm, fw), lambda i: (i, 0)),
            pl.BlockSpec((tm, gw), lambda i: (i, 0)),
        ],
        out_shape=[
            jax.ShapeDtypeStruct((t, fw), BF16),
            jax.ShapeDtypeStruct((t, fw), BF16),
            jax.ShapeDtypeStruct((t, gw), BF16),
        ],
        scratch_shapes=[pltpu.VMEM((tm, nin), F32)],
        compiler_params=pltpu.CompilerParams(
            dimension_semantics=("arbitrary",), vmem_limit_bytes=VMEM_LIMIT_BYTES),
        name="mix_in",
    )(x2, g_mix, w_in, wcs, g_sgu, w_sp, b_full)


def _stage2_matrices(n1, n2):
    n = n1 * n2
    assert n & (n - 1) == 0
    k1 = jnp.arange(n1, dtype=jnp.int32)[:, None, None]
    k2 = jnp.arange(n2, dtype=jnp.int32)[None, :, None]
    s2 = jnp.arange(n2, dtype=jnp.int32)[None, None, :]
    alpha = jnp.bitwise_and(s2 * k1, n - 1).astype(F32) * (2.0 * math.pi / n)
    beta = jnp.bitwise_and(s2 * k2, n2 - 1).astype(F32) * (2.0 * math.pi / n2)
    ca, sa, cb, sb = jnp.cos(alpha), jnp.sin(alpha), jnp.cos(beta), jnp.sin(beta)
    g = jnp.concatenate([ca * cb - sa * sb, -(sa * cb + ca * sb)], axis=-1) * (1.0 / math.sqrt(n))
    return g.astype(BF16)


def _fft_pos(xr, xi):
    n = len(xr)
    if n == 1:
        return xr, xi
    er, ei = _fft_pos(xr[0::2], xi[0::2])
    odr, odi = _fft_pos(xr[1::2], xi[1::2])
    outr, outi = [None] * n, [None] * n
    half = n // 2
    for k in range(half):
        a, b = odr[k], odi[k]
        if k == 0:
            tr, ti = a, b
        elif 4 * k == n:
            tr, ti = -b, a
        elif 8 * k == n:
            tr, ti = (a - b) * math.sqrt(0.5), (a + b) * math.sqrt(0.5)
        elif 8 * k == 3 * n:
            tr, ti = (a + b) * -math.sqrt(0.5), (a - b) * math.sqrt(0.5)
        else:
            c, s = math.cos(2.0 * math.pi * k / n), math.sin(2.0 * math.pi * k / n)
            tr, ti = a * c - b * s, b * c + a * s
        outr[k], outi[k] = er[k] + tr, ei[k] + ti
        outr[k + half], outi[k + half] = er[k] - tr, ei[k] - ti
    return outr, outi


def _seq_dft_kernel(g_ref, p_ref, q_ref, o_ref, t_ref, r_ref, *, n1, n2, pitch):
    tc = p_ref.shape[-1]
    nlt = tc // LANES
    rb = 16

    def stage1(i, carry):
        r0 = pl.multiple_of(i * rb, rb)
        for j in range(nlt):
            lanes = slice(j * LANES, (j + 1) * LANES)
            zr = [p_ref[0, pl.ds(s1 * n2 + r0, rb), lanes].astype(F32) for s1 in range(n1)]
            zi = [q_ref[0, pl.ds(s1 * n2 + r0, rb), lanes].astype(F32) for s1 in range(n1)]
            tr, ti = _fft_pos(zr, zi)
            for k1 in range(n1):
                t_ref[k1, pl.ds(r0, rb), lanes] = tr[k1].astype(BF16)
                t_ref[k1, pl.ds(n2 + r0, rb), lanes] = ti[k1].astype(BF16)
        return carry

    lax.fori_loop(0, n2 // rb, stage1, 0)

    for k1 in range(n1):
        res = jnp.dot(g_ref[k1], t_ref[k1], preferred_element_type=F32)
        for j in range(nlt):
            r_ref[j, k1 * pitch:k1 * pitch + n2, :] = res[:, j * LANES:(j + 1) * LANES]

    def interleave(k2, carry):
        row = pl.multiple_of(k2 * n1, n1)
        for j in range(nlt):
            blk = r_ref[j, pl.ds(k2, n1, stride=pitch), :]
            o_ref[0, pl.ds(row, n1), j * LANES:(j + 1) * LANES] = blk.astype(o_ref.dtype)
        return carry

    lax.fori_loop(0, n2, interleave, 0, unroll=8)


def _seq_dft(p, q, *, batch, n1, tc):
    t, w = p.shape
    nj = w // tc
    s = t // batch
    n2 = s // n1
    pitch = n2 + 8
    g2 = _stage2_matrices(n1, n2)
    blk = pl.BlockSpec((1, s, tc), lambda b, j: (b, 0, j))
    yf = pl.pallas_call(
        functools.partial(_seq_dft_kernel, n1=n1, n2=n2, pitch=pitch),
        grid=(batch, nj),
        in_specs=[pl.BlockSpec(g2.shape, lambda b, j: (0, 0, 0), pipeline_mode=pl.Buffered(1)),
                  blk, blk],
        out_specs=blk,
        out_shape=jax.ShapeDtypeStruct((batch, s, w), BF16),
        scratch_shapes=[pltpu.VMEM((n1, 2 * n2, tc), BF16),
                        pltpu.VMEM((tc // LANES, n1 * pitch, LANES), F32)],
        compiler_params=pltpu.CompilerParams(
            dimension_semantics=("arbitrary", "arbitrary"), vmem_limit_bytes=VMEM_LIMIT_BYTES),
        name="seq_dft",
    )(g2, p.reshape(batch, s, w), q.reshape(batch, s, w))
    return yf.reshape(t, w)


def _out_proj_kernel(x_ref, yf_ref, yg_ref, wo_ref, gffn_ref, x1_ref, h2_ref):
    fw = yf_ref.shape[1]
    for c in range(x1_ref.shape[1] // MXU_WIDTH):
        cols = slice(c * MXU_WIDTH, (c + 1) * MXU_WIDTH)
        x1_ref[:, cols] = (x_ref[:, cols]
                           + jnp.dot(yf_ref[...], wo_ref[:fw, cols].astype(BF16), preferred_element_type=F32)
                           + jnp.dot(yg_ref[...], wo_ref[fw:, cols].astype(BF16), preferred_element_type=F32))
    h2_ref[...] = _rms(x1_ref[...], gffn_ref[...]).astype(BF16)


def _out_proj(x2, yf, yg, w_out, g_ffn, *, tm):
    t, d = x2.shape
    fw = yf.shape[1]
    gw = yg.shape[1]
    row = lambda i: (i, 0)
    const = lambda i: (0, 0)
    return pl.pallas_call(
        _out_proj_kernel,
        grid=(t // tm,),
        in_specs=[
            pl.BlockSpec((tm, d), row),
            pl.BlockSpec((tm, fw), row),
            pl.BlockSpec((tm, gw), row),
            pl.BlockSpec((fw + gw, d), const, pipeline_mode=pl.Buffered(1)),
            pl.BlockSpec((1, d), const),
        ],
        out_specs=[pl.BlockSpec((tm, d), row), pl.BlockSpec((tm, d), row)],
        out_shape=[jax.ShapeDtypeStruct((t, d), F32), jax.ShapeDtypeStruct((t, d), BF16)],
        compiler_params=pltpu.CompilerParams(
            dimension_semantics=("arbitrary",), vmem_limit_bytes=VMEM_LIMIT_BYTES),
        name="out_proj",
    )(x2, yf, yg, w_out, g_ffn)


def _ffn_kernel(x1_hbm, h2_ref, wg_ref, wu_ref, wd_ref, gfin_ref, o_ref, t_ref, sem):
    i = pl.program_id(0)
    j = pl.program_id(1)
    nj = pl.num_programs(1) - 1
    tm = o_ref.shape[0]
    slot = j % 2
    x1_copy = pltpu.make_async_copy(x1_hbm.at[pl.ds(i * tm, tm), :], o_ref, sem)

    def gated(dst_slot):
        for c in range(wg_ref.shape[1] // MXU_WIDTH):
            cols = slice(c * MXU_WIDTH, (c + 1) * MXU_WIDTH)
            g = jnp.dot(h2_ref[...], wg_ref[:, cols].astype(BF16), preferred_element_type=F32)
            u = jnp.dot(h2_ref[...], wu_ref[:, cols].astype(BF16), preferred_element_type=F32)
            t_ref[dst_slot, :, cols] = (g * jax.nn.sigmoid(g) * u).astype(BF16)

    def down(src_slot):
        for c in range(o_ref.shape[1] // MXU_WIDTH):
            cols = slice(c * MXU_WIDTH, (c + 1) * MXU_WIDTH)
            o_ref[:, cols] += jnp.dot(t_ref[src_slot], wd_ref[:, cols].astype(BF16), preferred_element_type=F32)

    @pl.when(j == 0)
    def _():
        x1_copy.start()
        gated(0)

    @pl.when(j == 1)
    def _():
        x1_copy.wait()

    @pl.when(jnp.logical_and(j > 0, j < nj))
    def _():
        down(1 - slot)
        gated(slot)

    @pl.when(j == nj)
    def _():
        down(1 - slot)
        o_ref[...] = _rms(o_ref[...], gfin_ref[...])


def _ffn(x1, h2, w_gate, w_up, w_down, g_fin, *, tm, tf):
    t, d = x1.shape
    dff = w_gate.shape[1]
    nj = dff // tf
    row = lambda i, j: (i, 0)
    return pl.pallas_call(
        _ffn_kernel,
        grid=(t // tm, nj + 1),
        in_specs=[
            pl.BlockSpec(memory_space=pl.ANY),
            pl.BlockSpec((tm, d), row),
            pl.BlockSpec((d, tf), lambda i, j: (0, jnp.minimum(j, nj - 1))),
            pl.BlockSpec((d, tf), lambda i, j: (0, jnp.minimum(j, nj - 1))),
            pl.BlockSpec((tf, d), lambda i, j: (jnp.maximum(j - 1, 0), 0)),
            pl.BlockSpec((1, d), lambda i, j: (0, 0)),
        ],
        out_specs=pl.BlockSpec((tm, d), row),
        out_shape=jax.ShapeDtypeStruct((t, d), F32),
        scratch_shapes=[pltpu.VMEM((2, tm, tf), BF16), pltpu.SemaphoreType.DMA(())],
        compiler_params=pltpu.CompilerParams(
            dimension_semantics=("arbitrary", "arbitrary"), vmem_limit_bytes=VMEM_LIMIT_BYTES),
        name="ffn",
    )(x1, h2, w_gate, w_up, w_down, g_fin)


def kernel(x, norm_mix, w_in, w_fourier, sgu_norm, w_spatial, b_spatial, w_out, norm_ffn,
           w_gate, w_up, w_down, norm_final):
    b, s, d = x.shape
    depth = w_in.shape[0]
    gw = sgu_norm.shape[1]
    nheads_g = w_spatial.shape[1]
    assert depth == 1, "only the single-layer block is implemented"
    l = 0
    x2 = x.reshape(b * s, d)
    wcs = _fold_fourier_weights(w_fourier[l])
    b_full = jnp.repeat(jnp.transpose(b_spatial[l]), gw // nheads_g, axis=1)
    p, q, yg = _mix_in(
        x2, norm_mix[l][None, :], w_in[l], wcs, sgu_norm[l][None, :],
        w_spatial[l].astype(BF16), b_full, tm=512)
    yf = _seq_dft(p, q, batch=b, n1=SEQ_OUTER, tc=256)
    x1, h2 = _out_proj(x2, yf, yg, w_out[l], norm_ffn[l][None, :], tm=512)
    out = _ffn(x1, h2, w_gate[l], w_up[l], w_down[l], norm_final[None, :], tm=1024, tf=512)
    return out.reshape(b, s, d)
```

```python
import functools
import math

import jax
import jax.numpy as jnp
from jax import lax
from jax.experimental import pallas as pl
from jax.experimental.pallas import tpu as pltpu

EPS = 1e-6
HEAD = 128
CHUNK = 128
F32 = jnp.float32
BF16 = jnp.bfloat16

VMEM_LIMIT_BYTES = 60000 * 1024
LANES = 128
MXU_WIDTH = 256
SEQ_OUTER = 16


def _dft_mats(n):
    idx = jnp.arange(n, dtype=jnp.int32)
    assert n & (n - 1) == 0
    ks = jnp.bitwise_and(idx[:, None] * idx[None, :], n - 1)
    ang = ks.astype(F32) * (2.0 * math.pi / n)
    scale = 1.0 / math.sqrt(n)
    return jnp.cos(ang) * scale, jnp.sin(ang) * scale


def _rms(x, g):
    ms = jnp.mean(x * x, axis=-1, keepdims=True)
    return x * lax.rsqrt(ms + EPS) * g


def _gelu_tanh(x):
    c = math.sqrt(2.0 / math.pi)
    return 0.5 * x * (1.0 + jnp.tanh(c * (x + 0.044715 * (x * x * x))))


def _cast_job(w, nsteps, index_map):
    rows, cols = w.shape
    chunk = rows // nsteps
    assert chunk * nsteps == rows and chunk % 16 == 0
    spec = pl.BlockSpec((chunk, cols), index_map)
    return spec, jax.ShapeDtypeStruct(w.shape, BF16)


def _fold_kernel(cd_ref, sd_ref, w_ref, o_ref):
    for hd in range(w_ref.shape[0]):
        w = w_ref[hd]
        wc = jnp.dot(cd_ref[...], w, preferred_element_type=F32, precision=lax.Precision.HIGHEST)
        ws = jnp.dot(sd_ref[...], w, preferred_element_type=F32, precision=lax.Precision.HIGHEST)
        o_ref[hd, :, :HEAD] = wc.astype(BF16)
        o_ref[hd, :, HEAD:] = ws.astype(BF16)


def _fold_fourier_weights(w_fourier):
    nh = w_fourier.shape[0]
    cd, sd = _dft_mats(HEAD)
    return pl.pallas_call(
        _fold_kernel,
        out_shape=jax.ShapeDtypeStruct((nh, HEAD, 2 * HEAD), BF16),
        name="fold_fourier_weights",
    )(cd, sd, w_fourier)


def _mix_in_kernel(x_ref, gmix_ref, win_ref, wcs_ref, gsgu_ref, wsp_ref, bsp_ref, cast_ref,
                   p_ref, q_ref, yg_ref, cast_out_ref, z_ref, *, fw, gw):
    cast_out_ref[...] = cast_ref[...].astype(BF16)
    tm = x_ref.shape[0]
    nheads_f = fw // HEAD
    nheads_g = gw // HEAD
    nchunk = tm // CHUNK

    h = _rms(x_ref[...], gmix_ref[...]).astype(BF16)
    z_ref[:, :gw] = jnp.dot(h, win_ref[:, fw + gw:].astype(BF16), preferred_element_type=F32)
    z_ref[:, gw:2 * gw] = jnp.dot(h, win_ref[:, fw:fw + gw].astype(BF16), preferred_element_type=F32)
    z_ref[:, 2 * gw:] = jnp.dot(h, win_ref[:, :fw].astype(BF16), preferred_element_type=F32)

    v = _rms(_gelu_tanh(z_ref[:, :gw]), gsgu_ref[...]).astype(BF16)
    for hd in range(nheads_g):
        cols = slice(hd * HEAD, (hd + 1) * HEAD)
        vcat = jnp.concatenate(
            [v[n * CHUNK:(n + 1) * CHUNK, cols] for n in range(nchunk)], axis=1)
        sv = jnp.dot(wsp_ref[hd], vcat, preferred_element_type=F32)
        bias = bsp_ref[:, cols]
        for n in range(nchunk):
            rows = slice(n * CHUNK, (n + 1) * CHUNK)
            u = _gelu_tanh(z_ref[rows, gw + hd * HEAD: gw + (hd + 1) * HEAD])
            yg_ref[rows, cols] = (u * (sv[:, n * HEAD:(n + 1) * HEAD] + bias)).astype(BF16)

    for hd in range(nheads_f):
        cols = slice(hd * HEAD, (hd + 1) * HEAD)
        a = z_ref[:, 2 * gw + hd * HEAD: 2 * gw + (hd + 1) * HEAD].astype(BF16)
        pq = jnp.dot(a, wcs_ref[hd], preferred_element_type=F32)
        p_ref[:, cols] = pq[:, :HEAD].astype(BF16)
        q_ref[:, cols] = pq[:, HEAD:].astype(BF16)


def _mix_in(x2, g_mix, w_in, wcs, g_sgu, w_sp, b_full, w_cast, *, tm):
    t, d = x2.shape
    cast_spec, cast_shape = _cast_job(w_cast, t // tm, lambda i: (i, 0))
    nin = w_in.shape[1]
    gw = g_sgu.shape[1]
    fw = nin - 2 * gw
    const2 = lambda i: (0, 0)
    const3 = lambda i: (0, 0, 0)
    single = pl.Buffered(1)
    return pl.pallas_call(
        functools.partial(_mix_in_kernel, fw=fw, gw=gw),
        grid=(t // tm,),
        in_specs=[
            pl.BlockSpec((tm, d), lambda i: (i, 0)),
            pl.BlockSpec((1, d), const2),
            pl.BlockSpec((d, nin), const2, pipeline_mode=single),
            pl.BlockSpec(wcs.shape, const3),
            pl.BlockSpec((1, gw), const2),
            pl.BlockSpec(w_sp.shape, const3),
            pl.BlockSpec(b_full.shape, const2),
            cast_spec,
        ],
        out_specs=[
            pl.BlockSpec((tm, fw), lambda i: (i, 0)),
            pl.BlockSpec((tm, fw), lambda i: (i, 0)),
            pl.BlockSpec((tm, gw), lambda i: (i, 0)),
            cast_spec,
        ],
        out_shape=[
            jax.ShapeDtypeStruct((t, fw), BF16),
            jax.ShapeDtypeStruct((t, fw), BF16),
            jax.ShapeDtypeStruct((t, gw), BF16),
            cast_shape,
        ],
        scratch_shapes=[pltpu.VMEM((tm, nin), F32)],
        compiler_params=pltpu.CompilerParams(
            dimension_semantics=("arbitrary",), vmem_limit_bytes=VMEM_LIMIT_BYTES),
        name="mix_in",
    )(x2, g_mix, w_in, wcs, g_sgu, w_sp, b_full, w_cast)


def _stage2_matrices(n1, n2):
    n = n1 * n2
    assert n & (n - 1) == 0
    k1 = jnp.arange(n1, dtype=jnp.int32)[:, None, None]
    k2 = jnp.arange(n2, dtype=jnp.int32)[None, :, None]
    s2 = jnp.arange(n2, dtype=jnp.int32)[None, None, :]
    alpha = jnp.bitwise_and(s2 * k1, n - 1).astype(F32) * (2.0 * math.pi / n)
    beta = jnp.bitwise_and(s2 * k2, n2 - 1).astype(F32) * (2.0 * math.pi / n2)
    ca, sa, cb, sb = jnp.cos(alpha), jnp.sin(alpha), jnp.cos(beta), jnp.sin(beta)
    g = jnp.concatenate([ca * cb - sa * sb, -(sa * cb + ca * sb)], axis=-1) * (1.0 / math.sqrt(n))
    return g.astype(BF16)


def _fft_pos(xr, xi):
    n = len(xr)
    if n == 1:
        return xr, xi
    er, ei = _fft_pos(xr[0::2], xi[0::2])
    odr, odi = _fft_pos(xr[1::2], xi[1::2])
    outr, outi = [None] * n, [None] * n
    half = n // 2
    for k in range(half):
        a, b = odr[k], odi[k]
        if k == 0:
            tr, ti = a, b
        elif 4 * k == n:
            tr, ti = -b, a
        elif 8 * k == n:
            tr, ti = (a - b) * math.sqrt(0.5), (a + b) * math.sqrt(0.5)
        elif 8 * k == 3 * n:
            tr, ti = (a + b) * -math.sqrt(0.5), (a - b) * math.sqrt(0.5)
        else:
            c, s = math.cos(2.0 * math.pi * k / n), math.sin(2.0 * math.pi * k / n)
            tr, ti = a * c - b * s, b * c + a * s
        outr[k], outi[k] = er[k] + tr, ei[k] + ti
        outr[k + half], outi[k + half] = er[k] - tr, ei[k] - ti
    return outr, outi


def _seq_dft_kernel(g_ref, p_ref, q_ref, cast_ref, o_ref, cast_out_ref, t_ref, r_ref, *, n1, n2, pitch):
    cast_out_ref[...] = cast_ref[...].astype(BF16)
    tc = p_ref.shape[-1]
    nlt = tc // LANES
    rb = 16

    def stage1(i, carry):
        r0 = pl.multiple_of(i * rb, rb)
        for j in range(nlt):
            lanes = slice(j * LANES, (j + 1) * LANES)
            zr = [p_ref[0, pl.ds(s1 * n2 + r0, rb), lanes].astype(F32) for s1 in range(n1)]
            zi = [q_ref[0, pl.ds(s1 * n2 + r0, rb), lanes].astype(F32) for s1 in range(n1)]
            tr, ti = _fft_pos(zr, zi)
            for k1 in range(n1):
                t_ref[k1, pl.ds(r0, rb), lanes] = tr[k1].astype(BF16)
                t_ref[k1, pl.ds(n2 + r0, rb), lanes] = ti[k1].astype(BF16)
        return carry

    lax.fori_loop(0, n2 // rb, stage1, 0)

    for k1 in range(n1):
        res = jnp.dot(g_ref[k1], t_ref[k1], preferred_element_type=F32)
        for j in range(nlt):
            r_ref[j, k1 * pitch:k1 * pitch + n2, :] = res[:, j * LANES:(j + 1) * LANES]

    def interleave(k2, carry):
        row = pl.multiple_of(k2 * n1, n1)
        for j in range(nlt):
            blk = r_ref[j, pl.ds(k2, n1, stride=pitch), :]
            o_ref[0, pl.ds(row, n1), j * LANES:(j + 1) * LANES] = blk.astype(o_ref.dtype)
        return carry

    lax.fori_loop(0, n2, interleave, 0, unroll=8)


def _seq_dft(p, q, w_cast, *, batch, n1, tc):
    t, w = p.shape
    nj = w // tc
    cast_spec, cast_shape = _cast_job(w_cast, batch * nj, lambda b, j: (b * nj + j, 0))
    s = t // batch
    n2 = s // n1
    pitch = n2 + 8
    g2 = _stage2_matrices(n1, n2)
    blk = pl.BlockSpec((1, s, tc), lambda b, j: (b, 0, j))
    yf, w_bf = pl.pallas_call(
        functools.partial(_seq_dft_kernel, n1=n1, n2=n2, pitch=pitch),
        grid=(batch, nj),
        in_specs=[pl.BlockSpec(g2.shape, lambda b, j: (0, 0, 0), pipeline_mode=pl.Buffered(1)),
                  blk, blk, cast_spec],
        out_specs=[blk, cast_spec],
        out_shape=[jax.ShapeDtypeStruct((batch, s, w), BF16), cast_shape],
        scratch_shapes=[pltpu.VMEM((n1, 2 * n2, tc), BF16),
                        pltpu.VMEM((tc // LANES, n1 * pitch, LANES), F32)],
        compiler_params=pltpu.CompilerParams(
            dimension_semantics=("arbitrary", "arbitrary"), vmem_limit_bytes=VMEM_LIMIT_BYTES),
        name="seq_dft",
    )(g2, p.reshape(batch, s, w), q.reshape(batch, s, w), w_cast)
    return yf.reshape(t, w), w_bf


def _out_proj_kernel(x_ref, yf_ref, yg_ref, wo_ref, gffn_ref, x1_ref, h2_ref):
    fw = yf_ref.shape[1]
    for c in range(x1_ref.shape[1] // MXU_WIDTH):
        cols = slice(c * MXU_WIDTH, (c + 1) * MXU_WIDTH)
        x1_ref[:, cols] = (x_ref[:, cols]
                           + jnp.dot(yf_ref[...], wo_ref[:fw, cols].astype(BF16), preferred_element_type=F32)
                           + jnp.dot(yg_ref[...], wo_ref[fw:, cols].astype(BF16), preferred_element_type=F32))
    h2_ref[...] = _rms(x1_ref[...], gffn_ref[...]).astype(BF16)


def _out_proj(x2, yf, yg, w_out, g_ffn, *, tm):
    t, d = x2.shape
    fw = yf.shape[1]
    gw = yg.shape[1]
    row = lambda i: (i, 0)
    const = lambda i: (0, 0)
    return pl.pallas_call(
        _out_proj_kernel,
        grid=(t // tm,),
        in_specs=[
            pl.BlockSpec((tm, d), row),
            pl.BlockSpec((tm, fw), row),
            pl.BlockSpec((tm, gw), row),
            pl.BlockSpec((fw + gw, d), const, pipeline_mode=pl.Buffered(1)),
            pl.BlockSpec((1, d), const),
        ],
        out_specs=[pl.BlockSpec((tm, d), row), pl.BlockSpec((tm, d), row)],
        out_shape=[jax.ShapeDtypeStruct((t, d), F32), jax.ShapeDtypeStruct((t, d), BF16)],
        compiler_params=pltpu.CompilerParams(
            dimension_semantics=("arbitrary",), vmem_limit_bytes=VMEM_LIMIT_BYTES),
        name="out_proj",
    )(x2, yf, yg, w_out, g_ffn)


def _ffn_kernel(x1_hbm, h2_ref, wg_ref, wu_ref, wd_ref, gfin_ref, o_ref, t_ref, sem):
    i = pl.program_id(0)
    j = pl.program_id(1)
    nj = pl.num_programs(1) - 1
    tm = o_ref.shape[0]
    slot = j % 2
    x1_copy = pltpu.make_async_copy(x1_hbm.at[pl.ds(i * tm, tm), :], o_ref, sem)

    def gated(dst_slot):
        for c in range(wg_ref.shape[1] // MXU_WIDTH):
            cols = slice(c * MXU_WIDTH, (c + 1) * MXU_WIDTH)
            g = jnp.dot(h2_ref[...], wg_ref[:, cols], preferred_element_type=F32)
            u = jnp.dot(h2_ref[...], wu_ref[:, cols], preferred_element_type=F32)
            t_ref[dst_slot, :, cols] = (g * jax.nn.sigmoid(g) * u).astype(BF16)

    def down(src_slot):
        for c in range(o_ref.shape[1] // MXU_WIDTH):
            cols = slice(c * MXU_WIDTH, (c + 1) * MXU_WIDTH)
            o_ref[:, cols] += jnp.dot(t_ref[src_slot], wd_ref[:, cols].astype(BF16), preferred_element_type=F32)

    @pl.when(j == 0)
    def _():
        x1_copy.start()
        gated(0)

    @pl.when(j == 1)
    def _():
        x1_copy.wait()

    @pl.when(jnp.logical_and(j > 0, j < nj))
    def _():
        down(1 - slot)
        gated(slot)

    @pl.when(j == nj)
    def _():
        down(1 - slot)
        o_ref[...] = _rms(o_ref[...], gfin_ref[...])


def _ffn(x1, h2, w_gate, w_up, w_down, g_fin, *, tm, tf):
    t, d = x1.shape
    dff = w_gate.shape[1]
    nj = dff // tf
    row = lambda i, j: (i, 0)
    return pl.pallas_call(
        _ffn_kernel,
        grid=(t // tm, nj + 1),
        in_specs=[
            pl.BlockSpec(memory_space=pl.ANY),
            pl.BlockSpec((tm, d), row),
            pl.BlockSpec((d, tf), lambda i, j: (0, jnp.minimum(j, nj - 1))),
            pl.BlockSpec((d, tf), lambda i, j: (0, jnp.minimum(j, nj - 1))),
            pl.BlockSpec((tf, d), lambda i, j: (jnp.maximum(j - 1, 0), 0)),
            pl.BlockSpec((1, d), lambda i, j: (0, 0)),
        ],
        out_specs=pl.BlockSpec((tm, d), row),
        out_shape=jax.ShapeDtypeStruct((t, d), F32),
        scratch_shapes=[pltpu.VMEM((2, tm, tf), BF16), pltpu.SemaphoreType.DMA(())],
        compiler_params=pltpu.CompilerParams(
            dimension_semantics=("arbitrary", "arbitrary"), vmem_limit_bytes=VMEM_LIMIT_BYTES),
        name="ffn",
    )(x1, h2, w_gate, w_up, w_down, g_fin)


def kernel(x, norm_mix, w_in, w_fourier, sgu_norm, w_spatial, b_spatial, w_out, norm_ffn,
           w_gate, w_up, w_down, norm_final):
    b, s, d = x.shape
    depth = w_in.shape[0]
    gw = sgu_norm.shape[1]
    nheads_g = w_spatial.shape[1]
    assert depth == 1, "only the single-layer block is implemented"
    l = 0
    x2 = x.reshape(b * s, d)
    wcs = _fold_fourier_weights(w_fourier[l])
    b_full = jnp.repeat(jnp.transpose(b_spatial[l]), gw // nheads_g, axis=1)
    p, q, yg, wg_bf = _mix_in(
        x2, norm_mix[l][None, :], w_in[l], wcs, sgu_norm[l][None, :],
        w_spatial[l].astype(BF16), b_full, w_gate[l], tm=512)
    yf, wu_bf = _seq_dft(p, q, w_up[l], batch=b, n1=SEQ_OUTER, tc=256)
    x1, h2 = _out_proj(x2, yf, yg, w_out[l], norm_ffn[l][None, :], tm=512)
    out = _ffn(x1, h2, wg_bf, wu_bf, w_down[l], norm_final[None, :], tm=1024, tf=512)
    return out.reshape(b, s, d)
```

```python
import functools
import math

import jax
import jax.numpy as jnp
from jax import lax
from jax.experimental import pallas as pl
from jax.experimental.pallas import tpu as pltpu

EPS = 1e-6
HEAD = 128
CHUNK = 128
F32 = jnp.float32
BF16 = jnp.bfloat16

VMEM_LIMIT_BYTES = 60000 * 1024
LANES = 128
MXU_WIDTH = 256
SEQ_OUTER = 16


def _rms(x, g):
    ms = jnp.mean(x * x, axis=-1, keepdims=True)
    return x * lax.rsqrt(ms + EPS) * g


def _gelu_tanh(x):
    c = math.sqrt(2.0 / math.pi)
    return 0.5 * x * (1.0 + jnp.tanh(c * (x + 0.044715 * (x * x * x))))


def _cast_job(w, nsteps, index_map):
    rows, cols = w.shape
    chunk = rows // nsteps
    assert chunk * nsteps == rows and chunk % 16 == 0
    spec = pl.BlockSpec((chunk, cols), index_map)
    return spec, jax.ShapeDtypeStruct(w.shape, BF16)


def _fold_kernel(w_ref, bsp_ref, wcs_ref, g_ref, bfull_ref, *, n1, n2):
    r = lax.broadcasted_iota(jnp.int32, (HEAD, HEAD), 0)
    c = lax.broadcasted_iota(jnp.int32, (HEAD, HEAD), 1)
    ang = jnp.bitwise_and(r * c, HEAD - 1).astype(F32) * (2.0 * math.pi / HEAD)
    cd = jnp.cos(ang) * (1.0 / math.sqrt(HEAD))
    sd = jnp.sin(ang) * (1.0 / math.sqrt(HEAD))
    for hd in range(w_ref.shape[0]):
        w = w_ref[hd]
        wc = jnp.dot(cd, w, preferred_element_type=F32, precision=lax.Precision.HIGHEST)
        ws = jnp.dot(sd, w, preferred_element_type=F32, precision=lax.Precision.HIGHEST)
        wcs_ref[hd, :, :HEAD] = wc.astype(BF16)
        wcs_ref[hd, :, HEAD:] = ws.astype(BF16)

    n = n1 * n2
    k2 = lax.broadcasted_iota(jnp.int32, (n2, n2), 0)
    s2 = lax.broadcasted_iota(jnp.int32, (n2, n2), 1)
    beta = jnp.bitwise_and(k2 * s2, n2 - 1).astype(F32) * (2.0 * math.pi / n2)
    cb, sb = jnp.cos(beta), jnp.sin(beta)
    k1 = lax.broadcasted_iota(jnp.int32, (n1, n2), 0)
    s2r = lax.broadcasted_iota(jnp.int32, (n1, n2), 1)
    alpha = jnp.bitwise_and(k1 * s2r, n - 1).astype(F32) * (2.0 * math.pi / n)
    ca, sa = jnp.cos(alpha), jnp.sin(alpha)
    scale = 1.0 / math.sqrt(n)
    for i in range(n1):
        ci, si = ca[i:i + 1, :], sa[i:i + 1, :]
        g_ref[i, :, :n2] = ((ci * cb - si * sb) * scale).astype(BF16)
        g_ref[i, :, n2:] = (-(si * cb + ci * sb) * scale).astype(BF16)

    bt = jnp.transpose(bsp_ref[...])
    for hd in range(bsp_ref.shape[0]):
        bfull_ref[:, hd * HEAD:(hd + 1) * HEAD] = jnp.broadcast_to(bt[:, hd:hd + 1], (CHUNK, HEAD))


def _fold_setup(w_fourier, b_spatial, *, n1, n2):
    nh = w_fourier.shape[0]
    ng = b_spatial.shape[0]
    return pl.pallas_call(
        functools.partial(_fold_kernel, n1=n1, n2=n2),
        out_shape=[jax.ShapeDtypeStruct((nh, HEAD, 2 * HEAD), BF16),
                   jax.ShapeDtypeStruct((n1, n2, 2 * n2), BF16),
                   jax.ShapeDtypeStruct((CHUNK, ng * HEAD), F32)],
        name="fold_setup",
    )(w_fourier, b_spatial)


def _mix_in_kernel(x_ref, gmix_ref, win_ref, wcs_ref, gsgu_ref, wsp_ref, bsp_ref, cast_ref,
                   p_ref, q_ref, yg_ref, cast_out_ref, z_ref, *, fw, gw):
    cast_out_ref[...] = cast_ref[...].astype(BF16)
    tm = x_ref.shape[0]
    nheads_f = fw // HEAD
    nheads_g = gw // HEAD
    nchunk = tm // CHUNK

    h = _rms(x_ref[...], gmix_ref[...]).astype(BF16)
    z_ref[:, :gw] = jnp.dot(h, win_ref[:, fw + gw:].astype(BF16), preferred_element_type=F32)
    z_ref[:, gw:2 * gw] = jnp.dot(h, win_ref[:, fw:fw + gw].astype(BF16), preferred_element_type=F32)
    z_ref[:, 2 * gw:] = jnp.dot(h, win_ref[:, :fw].astype(BF16), preferred_element_type=F32)

    v = _rms(_gelu_tanh(z_ref[:, :gw]), gsgu_ref[...]).astype(BF16)
    for hd in range(nheads_g):
        cols = slice(hd * HEAD, (hd + 1) * HEAD)
        vcat = jnp.concatenate(
            [v[n * CHUNK:(n + 1) * CHUNK, cols] for n in range(nchunk)], axis=1)
        sv = jnp.dot(wsp_ref[hd].astype(BF16), vcat, preferred_element_type=F32)
        bias = bsp_ref[:, cols]
        for n in range(nchunk):
            rows = slice(n * CHUNK, (n + 1) * CHUNK)
            u = _gelu_tanh(z_ref[rows, gw + hd * HEAD: gw + (hd + 1) * HEAD])
            yg_ref[rows, cols] = (u * (sv[:, n * HEAD:(n + 1) * HEAD] + bias)).astype(BF16)

    for hd in range(nheads_f):
        cols = slice(hd * HEAD, (hd + 1) * HEAD)
        a = z_ref[:, 2 * gw + hd * HEAD: 2 * gw + (hd + 1) * HEAD].astype(BF16)
        pq = jnp.dot(a, wcs_ref[hd], preferred_element_type=F32)
        p_ref[:, cols] = pq[:, :HEAD].astype(BF16)
        q_ref[:, cols] = pq[:, HEAD:].astype(BF16)


def _mix_in(x2, g_mix, w_in, wcs, g_sgu, w_sp, b_full, w_cast, *, tm):
    t, d = x2.shape
    cast_spec, cast_shape = _cast_job(w_cast, t // tm, lambda i: (i, 0))
    nin = w_in.shape[1]
    gw = g_sgu.shape[1]
    fw = nin - 2 * gw
    const2 = lambda i: (0, 0)
    const3 = lambda i: (0, 0, 0)
    single = pl.Buffered(1)
    return pl.pallas_call(
        functools.partial(_mix_in_kernel, fw=fw, gw=gw),
        grid=(t // tm,),
        in_specs=[
            pl.BlockSpec((tm, d), lambda i: (i, 0)),
            pl.BlockSpec((1, d), const2),
            pl.BlockSpec((d, nin), const2, pipeline_mode=single),
            pl.BlockSpec(wcs.shape, const3),
            pl.BlockSpec((1, gw), const2),
            pl.BlockSpec(w_sp.shape, const3),
            pl.BlockSpec(b_full.shape, const2),
            cast_spec,
        ],
        out_specs=[
            pl.BlockSpec((tm, fw), lambda i: (i, 0)),
            pl.BlockSpec((tm, fw), lambda i: (i, 0)),
            pl.BlockSpec((tm, gw), lambda i: (i, 0)),
            cast_spec,
        ],
        out_shape=[
            jax.ShapeDtypeStruct((t, fw), BF16),
            jax.ShapeDtypeStruct((t, fw), BF16),
            jax.ShapeDtypeStruct((t, gw), BF16),
            cast_shape,
        ],
        scratch_shapes=[pltpu.VMEM((tm, nin), F32)],
        compiler_params=pltpu.CompilerParams(
            dimension_semantics=("arbitrary",), vmem_limit_bytes=VMEM_LIMIT_BYTES),
        name="mix_in",
    )(x2, g_mix, w_in, wcs, g_sgu, w_sp, b_full, w_cast)


def _fft_pos(xr, xi):
    n = len(xr)
    if n == 1:
        return xr, xi
    er, ei = _fft_pos(xr[0::2], xi[0::2])
    odr, odi = _fft_pos(xr[1::2], xi[1::2])
    outr, outi = [None] * n, [None] * n
    half = n // 2
    for k in range(half):
        a, b = odr[k], odi[k]
        if k == 0:
            tr, ti = a, b
        elif 4 * k == n:
            tr, ti = -b, a
        elif 8 * k == n:
            tr, ti = (a - b) * math.sqrt(0.5), (a + b) * math.sqrt(0.5)
        elif 8 * k == 3 * n:
            tr, ti = (a + b) * -math.sqrt(0.5), (a - b) * math.sqrt(0.5)
        else:
            c, s = math.cos(2.0 * math.pi * k / n), math.sin(2.0 * math.pi * k / n)
            tr, ti = a * c - b * s, b * c + a * s
        outr[k], outi[k] = er[k] + tr, ei[k] + ti
        outr[k + half], outi[k + half] = er[k] - tr, ei[k] - ti
    return outr, outi


def _seq_dft_kernel(g_ref, p_ref, q_ref, cast_ref, o_ref, cast_out_ref, t_ref, r_ref, *, n1, n2, pitch):
    cast_out_ref[...] = cast_ref[...].astype(BF16)
    tc = p_ref.shape[-1]
    nlt = tc // LANES
    rb = 16

    def stage1(i, carry):
        r0 = pl.multiple_of(i * rb, rb)
        for j in range(nlt):
            lanes = slice(j * LANES, (j + 1) * LANES)
            zr = [p_ref[0, pl.ds(s1 * n2 + r0, rb), lanes].astype(F32) for s1 in range(n1)]
            zi = [q_ref[0, pl.ds(s1 * n2 + r0, rb), lanes].astype(F32) for s1 in range(n1)]
            tr, ti = _fft_pos(zr, zi)
            for k1 in range(n1):
                t_ref[k1, pl.ds(r0, rb), lanes] = tr[k1].astype(BF16)
                t_ref[k1, pl.ds(n2 + r0, rb), lanes] = ti[k1].astype(BF16)
        return carry

    lax.fori_loop(0, n2 // rb, stage1, 0)

    for k1 in range(n1):
        res = jnp.dot(g_ref[k1], t_ref[k1], preferred_element_type=F32)
        for j in range(nlt):
            r_ref[j, k1 * pitch:k1 * pitch + n2, :] = res[:, j * LANES:(j + 1) * LANES]

    def interleave(k2, carry):
        row = pl.multiple_of(k2 * n1, n1)
        for j in range(nlt):
            blk = r_ref[j, pl.ds(k2, n1, stride=pitch), :]
            o_ref[0, pl.ds(row, n1), j * LANES:(j + 1) * LANES] = blk.astype(o_ref.dtype)
        return carry

    lax.fori_loop(0, n2, interleave, 0, unroll=8)


def _seq_dft(p, q, g2, w_cast, *, batch, n1, tc):
    t, w = p.shape
    nj = w // tc
    cast_spec, cast_shape = _cast_job(w_cast, batch * nj, lambda b, j: (b * nj + j, 0))
    s = t // batch
    n2 = s // n1
    pitch = n2 + 8
    blk = pl.BlockSpec((1, s, tc), lambda b, j: (b, 0, j))
    yf, w_bf = pl.pallas_call(
        functools.partial(_seq_dft_kernel, n1=n1, n2=n2, pitch=pitch),
        grid=(batch, nj),
        in_specs=[pl.BlockSpec(g2.shape, lambda b, j: (0, 0, 0), pipeline_mode=pl.Buffered(1)),
                  blk, blk, cast_spec],
        out_specs=[blk, cast_spec],
        out_shape=[jax.ShapeDtypeStruct((batch, s, w), BF16), cast_shape],
        scratch_shapes=[pltpu.VMEM((n1, 2 * n2, tc), BF16),
                        pltpu.VMEM((tc // LANES, n1 * pitch, LANES), F32)],
        compiler_params=pltpu.CompilerParams(
            dimension_semantics=("arbitrary", "arbitrary"), vmem_limit_bytes=VMEM_LIMIT_BYTES),
        name="seq_dft",
    )(g2, p.reshape(batch, s, w), q.reshape(batch, s, w), w_cast)
    return yf.reshape(t, w), w_bf


def _out_proj_kernel(x_ref, yf_ref, yg_ref, wo_ref, gffn_ref, x1_ref, h2_ref):
    fw = yf_ref.shape[1]
    for c in range(x1_ref.shape[1] // MXU_WIDTH):
        cols = slice(c * MXU_WIDTH, (c + 1) * MXU_WIDTH)
        x1_ref[:, cols] = (x_ref[:, cols]
                           + jnp.dot(yf_ref[...], wo_ref[:fw, cols].astype(BF16), preferred_element_type=F32)
                           + jnp.dot(yg_ref[...], wo_ref[fw:, cols].astype(BF16), preferred_element_type=F32))
    h2_ref[...] = _rms(x1_ref[...], gffn_ref[...]).astype(BF16)


def _out_proj(x2, yf, yg, w_out, g_ffn, *, tm):
    t, d = x2.shape
    fw = yf.shape[1]
    gw = yg.shape[1]
    row = lambda i: (i, 0)
    const = lambda i: (0, 0)
    return pl.pallas_call(
        _out_proj_kernel,
        grid=(t // tm,),
        in_specs=[
            pl.BlockSpec((tm, d), row),
            pl.BlockSpec((tm, fw), row),
            pl.BlockSpec((tm, gw), row),
            pl.BlockSpec((fw + gw, d), const, pipeline_mode=pl.Buffered(1)),
            pl.BlockSpec((1, d), const),
        ],
        out_specs=[pl.BlockSpec((tm, d), row), pl.BlockSpec((tm, d), row)],
        out_shape=[jax.ShapeDtypeStruct((t, d), F32), jax.ShapeDtypeStruct((t, d), BF16)],
        compiler_params=pltpu.CompilerParams(
            dimension_semantics=("arbitrary",), vmem_limit_bytes=VMEM_LIMIT_BYTES),
        name="out_proj",
    )(x2, yf, yg, w_out, g_ffn)


def _ffn_kernel(x1_hbm, h2_ref, wg_ref, wu_ref, wd_ref, gfin_ref, o_ref, t_ref, sem):
    i = pl.program_id(0)
    j = pl.program_id(1)
    nj = pl.num_programs(1) - 1
    tm = o_ref.shape[0]
    slot = j % 2
    x1_copy = pltpu.make_async_copy(x1_hbm.at[pl.ds(i * tm, tm), :], o_ref, sem)

    def gated(dst_slot):
        for c in range(wg_ref.shape[1] // MXU_WIDTH):
            cols = slice(c * MXU_WIDTH, (c + 1) * MXU_WIDTH)
            g = jnp.dot(h2_ref[...], wg_ref[:, cols], preferred_element_type=F32)
            u = jnp.dot(h2_ref[...], wu_ref[:, cols], preferred_element_type=F32)
            t_ref[dst_slot, :, cols] = (g * jax.nn.sigmoid(g) * u).astype(BF16)

    def down(src_slot):
        for c in range(o_ref.shape[1] // MXU_WIDTH):
            cols = slice(c * MXU_WIDTH, (c + 1) * MXU_WIDTH)
            o_ref[:, cols] += jnp.dot(t_ref[src_slot], wd_ref[:, cols].astype(BF16), preferred_element_type=F32)

    @pl.when(j == 0)
    def _():
        x1_copy.start()
        gated(0)

    @pl.when(j == 1)
    def _():
        x1_copy.wait()

    @pl.when(jnp.logical_and(j > 0, j < nj))
    def _():
        down(1 - slot)
        gated(slot)

    @pl.when(j == nj)
    def _():
        down(1 - slot)
        o_ref[...] = _rms(o_ref[...], gfin_ref[...])


def _ffn(x1, h2, w_gate, w_up, w_down, g_fin, *, tm, tf):
    t, d = x1.shape
    dff = w_gate.shape[1]
    nj = dff // tf
    row = lambda i, j: (i, 0)
    return pl.pallas_call(
        _ffn_kernel,
        grid=(t // tm, nj + 1),
        in_specs=[
            pl.BlockSpec(memory_space=pl.ANY),
            pl.BlockSpec((tm, d), row),
            pl.BlockSpec((d, tf), lambda i, j: (0, jnp.minimum(j, nj - 1))),
            pl.BlockSpec((d, tf), lambda i, j: (0, jnp.minimum(j, nj - 1))),
            pl.BlockSpec((tf, d), lambda i, j: (jnp.maximum(j - 1, 0), 0)),
            pl.BlockSpec((1, d), lambda i, j: (0, 0)),
        ],
        out_specs=pl.BlockSpec((tm, d), row),
        out_shape=jax.ShapeDtypeStruct((t, d), F32),
        scratch_shapes=[pltpu.VMEM((2, tm, tf), BF16), pltpu.SemaphoreType.DMA(())],
        compiler_params=pltpu.CompilerParams(
            dimension_semantics=("arbitrary", "arbitrary"), vmem_limit_bytes=VMEM_LIMIT_BYTES),
        name="ffn",
    )(x1, h2, w_gate, w_up, w_down, g_fin)


def kernel(x, norm_mix, w_in, w_fourier, sgu_norm, w_spatial, b_spatial, w_out, norm_ffn,
           w_gate, w_up, w_down, norm_final):
    b, s, d = x.shape
    depth = w_in.shape[0]
    assert depth == 1, "only the single-layer block is implemented"
    l = 0
    x2 = x.reshape(b * s, d)
    wcs, g2, b_full = _fold_setup(w_fourier[l], b_spatial[l], n1=SEQ_OUTER, n2=s // SEQ_OUTER)
    p, q, yg, wg_bf = _mix_in(
        x2, norm_mix[l][None, :], w_in[l], wcs, sgu_norm[l][None, :],
        w_spatial[l], b_full, w_gate[l], tm=512)
    yf, wu_bf = _seq_dft(p, q, g2, w_up[l], batch=b, n1=SEQ_OUTER, tc=256)
    x1, h2 = _out_proj(x2, yf, yg, w_out[l], norm_ffn[l][None, :], tm=512)
    out = _ffn(x1, h2, wg_bf, wu_bf, w_down[l], norm_final[None, :], tm=1024, tf=512)
    return out.reshape(b, s, d)
```
